```python
import math
import jax, jax.numpy as jnp
from jax import lax
import numpy as np

D_MODEL = 1024
BATCH = 2
SEQ = 16384
DEPTH = 2

HEAD_DIM = 64
MOBA_HEADS = 8
MOBA_BLOCK = 256
MOBA_TOPK = 3
MOBA_Q_CHUNK = 64
DIL_HEADS = 8
DIL_PATTERNS = ((128, 1), (512, 4), (2048, 16))
POOL_WINDOWS = (2, 4, 8, 16)
POOL_GROUPS = 4
POOL_WIDTH = 512
POOL_GROUP_DIM = POOL_WIDTH // POOL_GROUPS

N_BRANCH = 3
D_A = MOBA_HEADS * HEAD_DIM
D_B = DIL_HEADS * HEAD_DIM
D_FF = 4 * D_MODEL
IN_WIDTHS = (D_A, D_A, D_A, D_B, D_B, D_B, POOL_WIDTH, N_BRANCH * D_MODEL)
IN_COLS = 3 * D_A + 3 * D_B + POOL_WIDTH + N_BRANCH * D_MODEL
RMS_EPS = 1e-6

kernel_name = "hybrid_moba_dilated_pool_gated_block"


def rmsnorm(x, g):
    xf = x.astype(jnp.float32)
    var = jnp.mean(xf * xf, axis=-1, keepdims=True)
    return (xf * lax.rsqrt(var + RMS_EPS)).astype(x.dtype) * g


def split_heads(t, n_heads):
    b, s, _ = t.shape
    return t.reshape(b, s, n_heads, HEAD_DIM).transpose(0, 2, 1, 3)


def merge_heads(t):
    b, h, s, hd = t.shape
    return t.transpose(0, 2, 1, 3).reshape(b, s, h * hd)


def moba_attention(q, k, v):
    b, h, s, hd = q.shape
    nb = -(-s // MOBA_BLOCK)
    pad = nb * MOBA_BLOCK - s
    kp = jnp.pad(k, ((0, 0), (0, 0), (0, pad), (0, 0)))
    vp = jnp.pad(v, ((0, 0), (0, 0), (0, pad), (0, 0)))
    kb = kp.reshape(b, h, nb, MOBA_BLOCK, hd)
    vb = vp.reshape(b, h, nb, MOBA_BLOCK, hd)
    kbar = jnp.mean(kb.astype(jnp.float32), axis=3).astype(k.dtype)
    topk = min(MOBA_TOPK, nb)
    scale = HEAD_DIM ** -0.5
    n_chunks = s // MOBA_Q_CHUNK
    qc = q.reshape(b, h, n_chunks, MOBA_Q_CHUNK, hd).transpose(2, 0, 1, 3, 4)
    bi = jnp.arange(b)[:, None, None, None]
    hi = jnp.arange(h)[None, :, None, None]
    n_sel = topk * MOBA_BLOCK

    def one_chunk(args):
        c, qb = args
        start = c * MOBA_Q_CHUNK
        blk = start // MOBA_BLOCK
        gs = jnp.einsum('bhqd,bhnd->bhqn', qb, kbar).astype(jnp.float32)
        gs = jnp.where(jnp.arange(nb) < blk, gs, -jnp.inf)
        _, gidx = lax.top_k(gs, topk)
        valid = gidx < blk
        kg = kb[bi, hi, gidx]
        vg = vb[bi, hi, gidx]
        s_sel = jnp.einsum('bhqd,bhqjkd->bhqjk', qb, kg).astype(jnp.float32) * scale
        s_sel = jnp.where(valid[..., None], s_sel, -jnp.inf).reshape(b, h, MOBA_Q_CHUNK, n_sel)
        k_own = lax.dynamic_slice_in_dim(kp, blk * MOBA_BLOCK, MOBA_BLOCK, axis=2)
        v_own = lax.dynamic_slice_in_dim(vp, blk * MOBA_BLOCK, MOBA_BLOCK, axis=2)
        s_own = jnp.einsum('bhqd,bhkd->bhqk', qb, k_own).astype(jnp.float32) * scale
        qpos = start + jnp.arange(MOBA_Q_CHUNK)
        kpos = blk * MOBA_BLOCK + jnp.arange(MOBA_BLOCK)
        s_own = jnp.where(kpos[None, :] <= qpos[:, None], s_own, -jnp.inf)
        p = jax.nn.softmax(jnp.concatenate([s_sel, s_own], axis=-1), axis=-1).astype(v.dtype)
        p_sel = p[..., :n_sel].reshape(b, h, MOBA_Q_CHUNK, topk, MOBA_BLOCK)
        o = (jnp.einsum('bhqjk,bhqjkd->bhqd', p_sel, vg)
             + jnp.einsum('bhqk,bhkd->bhqd', p[..., n_sel:], v_own))
        return o

    o = lax.map(one_chunk, (jnp.arange(n_chunks), qc))
    return o.transpose(1, 2, 0, 3, 4).reshape(b, h, s, hd)


def dilated_pattern(q, k, v, window, dilation):
    b, h, s, hd = q.shape
    band = window // dilation
    L = s // dilation
    nb = -(-L // band)
    Lp = nb * band
    scale = HEAD_DIM ** -0.5

    def strided(t):
        t = t.reshape(b, h, L, dilation, hd).transpose(0, 1, 3, 2, 4)
        t = jnp.pad(t, ((0, 0), (0, 0), (0, 0), (0, Lp - L), (0, 0)))
        return t.reshape(b, h, dilation, nb, band, hd)

    def with_prev(t):
        prev = jnp.pad(t, ((0, 0), (0, 0), (0, 0), (1, 0), (0, 0), (0, 0)))[:, :, :, :-1]
        return jnp.concatenate([prev, t], axis=4)

    qs = strided(q)
    kw = with_prev(strided(k))
    vw = with_prev(strided(v))
    sc = jnp.einsum('bhrnqd,bhrnkd->bhrnqk', qs, kw).astype(jnp.float32) * scale
    i = jnp.arange(band)[:, None]
    j = jnp.arange(2 * band)[None, :]
    dist = band + i - j
    blk = jnp.arange(nb)[:, None, None]
    ok = (dist >= 0) & (dist <= band) & ((blk > 0) | (j >= band))
    sc = jnp.where(ok, sc, -jnp.inf)
    lse = jax.nn.logsumexp(sc, axis=-1)
    p = jnp.exp(sc - lse[..., None]).astype(v.dtype)
    o = jnp.einsum('bhrnqk,bhrnkd->bhrnqd', p, vw)
    o = o.reshape(b, h, dilation, Lp, hd)[:, :, :, :L].transpose(0, 1, 3, 2, 4).reshape(b, h, s, hd)
    lse = lse.reshape(b, h, dilation, Lp)[:, :, :, :L].transpose(0, 1, 3, 2).reshape(b, h, s)
    return o, lse


def dilated_attention(q, k, v):
    outs, lses = [], []
    for window, dilation in DIL_PATTERNS:
        o, lse = dilated_pattern(q, k, v, window, dilation)
        outs.append(o)
        lses.append(lse)
    wts = jax.nn.softmax(jnp.stack(lses, axis=0), axis=0).astype(q.dtype)
    return jnp.einsum('pbhs,pbhsd->bhsd', wts, jnp.stack(outs, axis=0))


def pool_mixer(u, w_pool, pool_scale):
    b, s, _ = u.shape
    ug = u.reshape(b, s, POOL_GROUPS, POOL_GROUP_DIM).astype(jnp.float32)
    cs = jnp.pad(jnp.cumsum(ug, axis=1), ((0, 0), (1, 0), (0, 0), (0, 0)))
    t = jnp.arange(s)
    pooled = []
    for g, w in enumerate(POOL_WINDOWS):
        csg = cs[:, :, g]
        lower = jnp.pad(csg, ((0, 0), (w - 1, 0), (0, 0)))[:, :s]
        cnt = jnp.minimum(t + 1, w).astype(jnp.float32)
        pooled.append((csg[:, 1:] - lower) / cnt[None, :, None])
    pooled = jnp.stack(pooled, axis=2) - ug
    y = jnp.einsum('bsgc,gcd->bsgd', pooled.astype(u.dtype), w_pool)
    return y.reshape(b, s, POOL_WIDTH) * pool_scale


def hybrid_layer(x, w_in, w_br_a, w_br_b, w_br_c, w_out, w_pool, pool_scale,
                 g_pre_mix, g_post_mix, g_pre_mlp, g_post_mlp, w_ff1, w_ff2):
    b, s, _ = x.shape
    h = rmsnorm(x, g_pre_mix)
    z = h @ w_in
    offsets = list(np.cumsum(IN_WIDTHS)[:-1])
    qa, ka, va, qb, kb, vb, uc, gz = jnp.split(z, offsets, axis=-1)
    oa = merge_heads(moba_attention(split_heads(qa, MOBA_HEADS), split_heads(ka, MOBA_HEADS),
                                    split_heads(va, MOBA_HEADS)))
    ob = merge_heads(dilated_attention(split_heads(qb, DIL_HEADS), split_heads(kb, DIL_HEADS),
                                       split_heads(vb, DIL_HEADS)))
    oc = pool_mixer(uc, w_pool, pool_scale)
    gates = jax.nn.sigmoid(gz.reshape(b, s, N_BRANCH, D_MODEL))
    merged = (gates[:, :, 0] * (oa @ w_br_a)
              + gates[:, :, 1] * (ob @ w_br_b)
              + gates[:, :, 2] * (oc @ w_br_c))
    x = x + rmsnorm(merged @ w_out, g_post_mix)
    hm = rmsnorm(x, g_pre_mlp)
    y = jnp.square(jax.nn.relu(hm @ w_ff1)) @ w_ff2
    return x + rmsnorm(y, g_post_mlp)


def setup_inputs(seed: int = 0) -> dict:
    key = jax.random.key(seed)
    ks = jax.random.split(key, 16)
    f32 = jnp.float32

    def nrm(k, shape, fan_in):
        return jax.random.normal(k, shape, f32) * (fan_in ** -0.5)

    return {
        "x": jax.random.normal(ks[0], (BATCH, SEQ, D_MODEL), f32),
        "w_in": nrm(ks[1], (DEPTH, D_MODEL, IN_COLS), D_MODEL),
        "w_br_a": nrm(ks[2], (DEPTH, D_A, D_MODEL), D_A),
        "w_br_b": nrm(ks[3], (DEPTH, D_B, D_MODEL), D_B),
        "w_br_c": nrm(ks[4], (DEPTH, POOL_WIDTH, D_MODEL), POOL_WIDTH),
        "w_out": nrm(ks[5], (DEPTH, D_MODEL, D_MODEL), D_MODEL),
        "w_pool": nrm(ks[6], (DEPTH, POOL_GROUPS, POOL_GROUP_DIM, POOL_GROUP_DIM), POOL_GROUP_DIM),
        "pool_scale": 0.5 + 0.1 * jax.random.normal(ks[7], (DEPTH, POOL_WIDTH), f32),
        "g_pre_mix": 1.0 + 0.05 * jax.random.normal(ks[8], (DEPTH, D_MODEL), f32),
        "g_post_mix": 1.0 + 0.05 * jax.random.normal(ks[9], (DEPTH, D_MODEL), f32),
        "g_pre_mlp": 1.0 + 0.05 * jax.random.normal(ks[10], (DEPTH, D_MODEL), f32),
        "g_post_mlp": 1.0 + 0.05 * jax.random.normal(ks[11], (DEPTH, D_MODEL), f32),
        "w_ff1": nrm(ks[12], (DEPTH, D_MODEL, D_FF), D_MODEL),
        "w_ff2": nrm(ks[13], (DEPTH, D_FF, D_MODEL), D_FF),
    }


def reference(x, w_in, w_br_a, w_br_b, w_br_c, w_out, w_pool, pool_scale,
              g_pre_mix, g_post_mix, g_pre_mlp, g_post_mlp, w_ff1, w_ff2):
    for l in range(DEPTH):
        x = hybrid_layer(x, w_in[l], w_br_a[l], w_br_b[l], w_br_c[l], w_out[l], w_pool[l],
                         pool_scale[l], g_pre_mix[l], g_post_mix[l], g_pre_mlp[l], g_post_mlp[l],
                         w_ff1[l], w_ff2[l])
    return x
```

```python
import functools

import jax
import jax.numpy as jnp
from jax import lax
from jax.experimental import pallas as pl
from jax.experimental.pallas import tpu as pltpu

F32 = jnp.float32
BF16 = jnp.bfloat16

D_MODEL = 1024
HEAD_DIM = 64
N_HEADS = 8
D_ATT = N_HEADS * HEAD_DIM
PAIR = 2 * HEAD_DIM
N_PAIRS = N_HEADS // 2
MOBA_BLOCK = 256
MOBA_TOPK = 3
DIL_PATTERNS = ((128, 1), (512, 4), (2048, 16))
DIL_BAND = 128
DIL_TILE = 2048
POOL_WINDOWS = (2, 4, 8, 16)
POOL_WIDTH = 512
POOL_GROUP_DIM = 128
POOL_HALO = 16
D_FF = 4 * D_MODEL
RMS_EPS = 1e-6
QK_SCALE = HEAD_DIM ** -0.5
NEG = -1e30

TOK_TILE = 512
VMEM_LIMIT = 56 * 1024 * 1024

NT_DIMS = (((1,), (1,)), ((), ()))


def _dot(a, b):
    return jnp.dot(a, b, preferred_element_type=F32)


def _dot_nt(a, b):
    return lax.dot_general(a, b, NT_DIMS, preferred_element_type=F32)


def _split_bf16(v):
    hi = v.astype(BF16)
    lo = (v - hi.astype(F32)).astype(BF16)
    return hi, lo


def _rms(x, g):
    var = jnp.mean(x * x, axis=-1, keepdims=True)
    return x * lax.rsqrt(var + RMS_EPS) * g


def _in_proj_kernel(x_ref, g_ref, wk_hi_ref, wk_lo_ref, wtok_ref, wqt_hi_ref, wqt_lo_ref, wvt_ref,
                    ka_ref, qkvb_ref, qt_ref, bias_ref, vt_ref, kbar_ref):
    t = pl.program_id(1)
    blocks_per_tile = TOK_TILE // MOBA_BLOCK

    @pl.when(t == 0)
    def _():
        kbar_ref[...] = jnp.zeros_like(kbar_ref)

    h = _rms(x_ref[0], g_ref[...])
    h_hi, h_lo = _split_bf16(h)

    ka = _dot(h_hi, wk_hi_ref[...]) + _dot(h_hi, wk_lo_ref[...]) + _dot(h_lo, wk_hi_ref[...])
    ka_ref[0] = ka.astype(BF16)
    for blk in range(blocks_per_tile):
        kbar_ref[pl.ds(t * blocks_per_tile + blk, 1), :] = jnp.mean(
            ka[blk * MOBA_BLOCK:(blk + 1) * MOBA_BLOCK], axis=0, keepdims=True)

    qkvb = _dot(h_hi, wtok_ref[...])
    qkvb_ref[0, :, 0:D_ATT] = qkvb[:, 0:D_ATT] * QK_SCALE
    qkvb_ref[0, :, D_ATT:3 * D_ATT] = qkvb[:, D_ATT:3 * D_ATT]

    qt = (_dot_nt(wqt_hi_ref[...], h_hi) + _dot_nt(wqt_lo_ref[...], h_hi)
          + _dot_nt(wqt_hi_ref[...], h_lo)) * QK_SCALE
    qt_ref[0] = qt.astype(BF16)
    vt_ref[0] = _dot_nt(wvt_ref[...], h_hi).astype(BF16)

    half = D_ATT // 2
    r = lax.broadcasted_iota(jnp.int32, (half, half), 0) // HEAD_DIM
    c = lax.broadcasted_iota(jnp.int32, (half, half), 1) // HEAD_DIM
    same_head = r == c
    n_blocks = kbar_ref.shape[0]
    blk_row = lax.broadcasted_iota(jnp.int32, (n_blocks, TOK_TILE), 0).astype(F32)
    q_blk = (t * blocks_per_tile
             + lax.broadcasted_iota(jnp.int32, (n_blocks, TOK_TILE), 1) // MOBA_BLOCK).astype(F32)
    valid = blk_row < q_blk
    for quad in range(2):
        kq = kbar_ref[:, quad * half:(quad + 1) * half]
        kb = jnp.where(same_head, jnp.concatenate([kq] * 4, axis=0), 0.0)
        kb_hi, kb_lo = _split_bf16(kb)
        q_hi, q_lo = _split_bf16(qt[quad * half:(quad + 1) * half])
        gs = _dot(kb_hi, q_hi) + _dot(kb_hi, q_lo) + _dot(kb_lo, q_hi)
        for hq in range(4):
            g = jnp.where(valid, gs[hq * n_blocks:(hq + 1) * n_blocks], -jnp.inf)
            sel = jnp.zeros(g.shape, jnp.bool_)
            for _ in range(MOBA_TOPK):
                m = jnp.max(g, axis=0, keepdims=True)
                idx = jnp.min(jnp.where(g == m, blk_row, float(n_blocks)), axis=0, keepdims=True)
                pick = blk_row == idx
                sel = jnp.logical_or(sel, pick)
                g = jnp.where(pick, -jnp.inf, g)
            head = quad * 4 + hq
            bias_ref[0, head * n_blocks:(head + 1) * n_blocks, :] = jnp.where(
                jnp.logical_and(sel, valid), 0.0, NEG)


def _in_proj(x, g, wk_hi, wk_lo, wtok, wqt_hi, wqt_lo, wvt):
    b, s, d = x.shape
    n_blocks = s // MOBA_BLOCK
    const = lambda shape: pl.BlockSpec(shape, lambda bi, ti: (0,) * len(shape))
    return pl.pallas_call(
        _in_proj_kernel,
        grid=(b, s // TOK_TILE),
        in_specs=[
            pl.BlockSpec((1, TOK_TILE, d), lambda bi, ti: (bi, ti, 0)),
            const((1, d)),
            const((d, D_ATT)), const((d, D_ATT)), const((d, 3 * D_ATT)),
            const((D_ATT, d)), const((D_ATT, d)), const((D_ATT, d)),
        ],
        out_specs=[
            pl.BlockSpec((1, TOK_TILE, D_ATT), lambda bi, ti: (bi, ti, 0)),
            pl.BlockSpec((1, TOK_TILE, 3 * D_ATT), lambda bi, ti: (bi, ti, 0)),
            pl.BlockSpec((1, D_ATT, TOK_TILE), lambda bi, ti: (bi, 0, ti)),
            pl.BlockSpec((1, N_HEADS * n_blocks, TOK_TILE), lambda bi, ti: (bi, 0, ti)),
            pl.BlockSpec((1, D_ATT, TOK_TILE), lambda bi, ti: (bi, 0, ti)),
        ],
        out_shape=[
            jax.ShapeDtypeStruct((b, s, D_ATT), BF16),
            jax.ShapeDtypeStruct((b, s, 3 * D_ATT), F32),
            jax.ShapeDtypeStruct((b, D_ATT, s), BF16),
            jax.ShapeDtypeStruct((b, N_HEADS * n_blocks, s), F32),
            jax.ShapeDtypeStruct((b, D_ATT, s), BF16),
        ],
        scratch_shapes=[pltpu.VMEM((n_blocks, D_ATT), F32)],
        compiler_params=pltpu.CompilerParams(
            dimension_semantics=("arbitrary", "arbitrary"), vmem_limit_bytes=VMEM_LIMIT),
        name="in_proj",
    )(x, g, wk_hi, wk_lo, wtok, wqt_hi, wqt_lo, wvt)


def _moba_kernel(qt_ref, bias_ref, k_ref, vt_ref, o_ref):
    i = pl.program_id(2)
    n_blocks = bias_ref.shape[1] // 2
    qt = qt_ref[0]
    zeros = jnp.zeros((HEAD_DIM, MOBA_BLOCK), BF16)
    qt_heads = (jnp.concatenate([qt[:HEAD_DIM], zeros], axis=0),
                jnp.concatenate([zeros, qt[HEAD_DIM:]], axis=0))

    def kv_block(j):
        start = pl.multiple_of(j * MOBA_BLOCK, MOBA_BLOCK)
        return k_ref[0, pl.ds(start, MOBA_BLOCK), :], vt_ref[0, :, pl.ds(start, MOBA_BLOCK)]

    k_i, vt_i = kv_block(i)
    key_pos = lax.broadcasted_iota(jnp.int32, (MOBA_BLOCK, MOBA_BLOCK), 0)
    qry_pos = lax.broadcasted_iota(jnp.int32, (MOBA_BLOCK, MOBA_BLOCK), 1)
    causal = key_pos <= qry_pos
    init = []
    for hd in range(2):
        s = jnp.where(causal, _dot(k_i, qt_heads[hd]), NEG)
        m = jnp.max(s, axis=0, keepdims=True)
        p = jnp.exp(s - m)
        l = jnp.sum(p, axis=0, keepdims=True)
        acc = _dot(vt_i[hd * HEAD_DIM:(hd + 1) * HEAD_DIM], p.astype(BF16))
        init += [m, l, acc]

    def body(j, carry):
        k_j, vt_j = kv_block(j)
        out = []
        for hd in range(2):
            m, l, acc = carry[3 * hd:3 * hd + 3]
            gate = bias_ref[0, pl.ds(hd * n_blocks + j, 1), :]
            s = _dot(k_j, qt_heads[hd])
            m_new = jnp.maximum(m, jnp.max(s, axis=0, keepdims=True) + gate)
            alpha = jnp.exp(m - m_new)
            p = jnp.exp(s - (m_new - gate))
            l = alpha * l + jnp.sum(p, axis=0, keepdims=True)
            acc = alpha * acc + _dot(vt_j[hd * HEAD_DIM:(hd + 1) * HEAD_DIM], p.astype(BF16))
            out += [m_new, l, acc]
        return tuple(out)

    fin = lax.fori_loop(0, i, body, tuple(init))
    ot = jnp.concatenate([fin[2] * (1.0 / fin[1]), fin[5] * (1.0 / fin[4])], axis=0)
    o_ref[0] = ot.T.astype(BF16)


def _moba(qt, bias, ka, vt):
    b, s, _ = ka.shape
    n_blocks = s // MOBA_BLOCK
    return pl.pallas_call(
        _moba_kernel,
        grid=(b, N_PAIRS, n_blocks),
        in_specs=[
            pl.BlockSpec((1, PAIR, MOBA_BLOCK), lambda bi, p, i: (bi, p, i)),
            pl.BlockSpec((1, 2 * n_blocks, MOBA_BLOCK), lambda bi, p, i: (bi, p, i)),
            pl.BlockSpec((1, s, PAIR), lambda bi, p, i: (bi, 0, p)),
            pl.BlockSpec((1, PAIR, s), lambda bi, p, i: (bi, p, 0)),
        ],
        out_specs=pl.BlockSpec((1, MOBA_BLOCK, PAIR), lambda bi, p, i: (bi, i, p)),
        out_shape=jax.ShapeDtypeStruct((b, s, D_ATT), BF16),
        compiler_params=pltpu.CompilerParams(
            dimension_semantics=("arbitrary", "arbitrary", "arbitrary"), vmem_limit_bytes=VMEM_LIMIT),
        name="moba",
    )(qt, bias, ka, vt)


def _dilated_kernel(q_ref, kc_ref, kp_ref, vc_ref, vp_ref, o_ref, o_sc, lse_sc):
    first_tile = pl.program_id(2) == 0
    lane = lax.broadcasted_iota(jnp.int32, (DIL_BAND, PAIR), 1)
    head_a = lane < HEAD_DIM
    qi = lax.broadcasted_iota(jnp.int32, (DIL_BAND, 2 * DIL_BAND), 0)
    kj = lax.broadcasted_iota(jnp.int32, (DIL_BAND, 2 * DIL_BAND), 1)
    band = jnp.where(jnp.logical_and(kj >= qi, kj <= qi + DIL_BAND), 0.0, NEG)
    band_first = jnp.where(kj >= DIL_BAND, band, NEG)

    def tile(pat, dil, q_start, k_lo, v_lo, k_hi_start, mask):
        rows = lambda ref, start: ref[0, pl.ds(start, DIL_BAND, stride=dil), :]
        q = rows(q_ref, q_start)
        k = jnp.concatenate([k_lo, rows(kc_ref, k_hi_start)], axis=0).astype(BF16)
        v = jnp.concatenate([v_lo, rows(vc_ref, k_hi_start)], axis=0).astype(BF16)
        outs, lses = [], []
        for hd in range(2):
            qh = jnp.where(head_a if hd == 0 else jnp.logical_not(head_a), q, 0.0).astype(BF16)
            s = _dot_nt(qh, k) + mask
            m = jnp.max(s, axis=-1, keepdims=True)
            p = jnp.exp(s - m)
            l = jnp.sum(p, axis=-1, keepdims=True)
            outs.append(_dot(p.astype(BF16), v) / l)
            lses.append(m + jnp.log(l))
        o_sc[pat, pl.ds(q_start, DIL_BAND, stride=dil), :] = jnp.where(head_a, outs[0], outs[1])
        lse_sc[pat, pl.ds(q_start, DIL_BAND, stride=dil), :] = jnp.where(head_a, lses[0], lses[1])

    for pat, (window, dil) in enumerate(DIL_PATTERNS):
        n_q = DIL_TILE // (DIL_BAND * dil)
        step = DIL_BAND * dil

        def residue(r, carry, pat=pat, dil=dil, n_q=n_q, step=step):
            prev_start = DIL_TILE - step + r
            k_lo = kp_ref[0, pl.ds(prev_start, DIL_BAND, stride=dil), :]
            v_lo = vp_ref[0, pl.ds(prev_start, DIL_BAND, stride=dil), :]
            tile(pat, dil, r, k_lo, v_lo, r, jnp.where(first_tile, band_first, band))

            def inner(qb, c):
                lo_start = (qb - 1) * step + r
                k_lo = kc_ref[0, pl.ds(lo_start, DIL_BAND, stride=dil), :]
                v_lo = vc_ref[0, pl.ds(lo_start, DIL_BAND, stride=dil), :]
                tile(pat, dil, qb * step + r, k_lo, v_lo, qb * step + r, band)
                return c

            if n_q > 1:
                lax.fori_loop(1, n_q, inner, 0)
            return carry

        lax.fori_loop(0, dil, residue, 0)

    lse = lse_sc[...]
    w = jnp.exp(lse - jnp.max(lse, axis=0, keepdims=True))
    o_ref[0] = (jnp.sum(w * o_sc[...], axis=0) / jnp.sum(w, axis=0)).astype(BF16)


def _dilated(qkvb):
    b, s, _ = qkvb.shape
    tile_spec = lambda col0, prev: pl.BlockSpec(
        (1, DIL_TILE, PAIR),
        (lambda bi, p, t: (bi, jnp.maximum(t - 1, 0), col0 + p)) if prev else (lambda bi, p, t: (bi, t, col0 + p)))
    return pl.pallas_call(
        _dilated_kernel,
        grid=(b, N_PAIRS, s // DIL_TILE),
        in_specs=[tile_spec(0, False),
                  tile_spec(N_PAIRS, False), tile_spec(N_PAIRS, True),
                  tile_spec(2 * N_PAIRS, False), tile_spec(2 * N_PAIRS, True)],
        out_specs=pl.BlockSpec((1, DIL_TILE, PAIR), lambda bi, p, t: (bi, t, p)),
        out_shape=jax.ShapeDtypeStruct((b, s, D_ATT), BF16),
        scratch_shapes=[pltpu.VMEM((len(DIL_PATTERNS), DIL_TILE, PAIR), F32),
                        pltpu.VMEM((len(DIL_PATTERNS), DIL_TILE, PAIR), F32)],
        compiler_params=pltpu.CompilerParams(
            dimension_semantics=("arbitrary", "arbitrary", "arbitrary"), vmem_limit_bytes=VMEM_LIMIT),
        name="dilated",
    )(qkvb, qkvb, qkvb, qkvb, qkvb)


def _merge_kernel(x_ref, xh_ref, oa_ref, ob_ref, gpre_ref, gpost_ref, wu_ref, wg_ref, wa_ref, wb_ref, wc_ref,
                  wout_ref, wpool_ref, pscale_ref, o_ref):
    t = pl.program_id(1)
    x = x_ref[0]
    h = _rms(x, gpre_ref[...]).astype(BF16)
    hh = _rms(xh_ref[0], gpre_ref[...]).astype(BF16)
    u = _dot(h, wu_ref[...])
    uh = jnp.where(t == 0, 0.0, _dot(hh, wu_ref[...]))
    u_ext = jnp.concatenate([uh, u], axis=0)
    pos = (t * TOK_TILE + lax.broadcasted_iota(jnp.int32, (TOK_TILE, POOL_GROUP_DIM), 0) + 1).astype(F32)
    oc = []
    for g, w in enumerate(POOL_WINDOWS):
        cols = slice(g * POOL_GROUP_DIM, (g + 1) * POOL_GROUP_DIM)
        acc = u_ext[:, cols]
        span = 1
        while span < w:
            acc = acc + pltpu.roll(acc, span, 0)
            span *= 2
        pooled = acc[POOL_HALO:] / jnp.minimum(pos, float(w)) - u[:, cols]
        oc.append(_dot(pooled.astype(BF16), wpool_ref[g]))
    oc = (jnp.concatenate(oc, axis=1) * pscale_ref[...]).astype(BF16)

    merged = None
    for br, (src, w_ref) in enumerate(((oa_ref[0], wa_ref), (ob_ref[0], wb_ref), (oc, wc_ref))):
        gz = _dot(h, wg_ref[:, br * D_MODEL:(br + 1) * D_MODEL])
        term = (1.0 / (1.0 + jnp.exp(-gz))) * _dot(src, w_ref[...])
        merged = term if merged is None else merged + term
    y = _dot(merged.astype(BF16), wout_ref[...])
    o_ref[0] = x + _rms(y, gpost_ref[...])


def _merge(x, oa, ob, gpre, gpost, wu, wg, wa, wb, wc, wout, wpool, pscale):
    b, s, d = x.shape
    const = lambda shape: pl.BlockSpec(shape, lambda bi, ti: (0,) * len(shape))
    halo_blocks = TOK_TILE // POOL_HALO
    return pl.pallas_call(
        _merge_kernel,
        grid=(b, s // TOK_TILE),
        in_specs=[
            pl.BlockSpec((1, TOK_TILE, d), lambda bi, ti: (bi, ti, 0)),
            pl.BlockSpec((1, POOL_HALO, d), lambda bi, ti: (bi, jnp.maximum(ti * halo_blocks - 1, 0), 0)),
            pl.BlockSpec((1, TOK_TILE, D_ATT), lambda bi, ti: (bi, ti, 0)),
            pl.BlockSpec((1, TOK_TILE, D_ATT), lambda bi, ti: (bi, ti, 0)),
            const((1, d)), const((1, d)),
            const((d, POOL_WIDTH)), const((d, 3 * d)),
            const((D_ATT, d)), const((D_ATT, d)), const((POOL_WIDTH, d)),
            const((d, d)), const((len(POOL_WINDOWS), POOL_GROUP_DIM, POOL_GROUP_DIM)), const((1, POOL_WIDTH)),
        ],
        out_specs=pl.BlockSpec((1, TOK_TILE, d), lambda bi, ti: (bi, ti, 0)),
        out_shape=jax.ShapeDtypeStruct((b, s, d), F32),
        compiler_params=pltpu.CompilerParams(
            dimension_semantics=("arbitrary", "arbitrary"), vmem_limit_bytes=VMEM_LIMIT),
        name="merge",
    )(x, x, oa, ob, gpre, gpost, wu, wg, wa, wb, wc, wout, wpool, pscale)


def _mlp_kernel(x_ref, gpre_ref, gpost_ref, w1_ref, w2_ref, o_ref):
    x = x_ref[0]
    h = _rms(x, gpre_ref[...]).astype(BF16)
    y = None
    for c in range(D_FF // D_MODEL):
        cols = slice(c * D_MODEL, (c + 1) * D_MODEL)
        a = jnp.maximum(_dot(h, w1_ref[:, cols]), 0.0)
        part = _dot((a * a).astype(BF16), w2_ref[cols, :])
        y = part if y is None else y + part
    o_ref[0] = x + _rms(y, gpost_ref[...])


def _mlp(x, gpre, gpost, w1, w2):
    b, s, d = x.shape
    const = lambda shape: pl.BlockSpec(shape, lambda bi, ti: (0,) * len(shape))
    return pl.pallas_call(
        _mlp_kernel,
        grid=(b, s // TOK_TILE),
        in_specs=[pl.BlockSpec((1, TOK_TILE, d), lambda bi, ti: (bi, ti, 0)),
                  const((1, d)), const((1, d)), const((d, D_FF)), const((D_FF, d))],
        out_specs=pl.BlockSpec((1, TOK_TILE, d), lambda bi, ti: (bi, ti, 0)),
        out_shape=jax.ShapeDtypeStruct((b, s, d), F32),
        compiler_params=pltpu.CompilerParams(
            dimension_semantics=("arbitrary", "arbitrary"), vmem_limit_bytes=VMEM_LIMIT),
        name="mlp",
    )(x, gpre, gpost, w1, w2)


def _layer(x, w_in, w_br_a, w_br_b, w_br_c, w_out, w_pool, pool_scale,
           g_pre_mix, g_post_mix, g_pre_mlp, g_post_mlp, w_ff1, w_ff2):
    col = lambda k: slice(k * D_ATT, (k + 1) * D_ATT)
    w_qa, w_ka, w_va = w_in[:, col(0)], w_in[:, col(1)], w_in[:, col(2)]
    wk_hi, wk_lo = _split_bf16(w_ka)
    wqt_hi, wqt_lo = _split_bf16(w_qa.T)
    wvt = w_va.T.astype(BF16)
    wtok = w_in[:, 3 * D_ATT:6 * D_ATT].astype(BF16)
    wu = w_in[:, 6 * D_ATT:6 * D_ATT + POOL_WIDTH].astype(BF16)
    wg = w_in[:, 6 * D_ATT + POOL_WIDTH:].astype(BF16)
    row = lambda v: v.reshape(1, -1)

    ka, qkvb, qt, bias, vt = _in_proj(x, row(g_pre_mix), wk_hi, wk_lo, wtok, wqt_hi, wqt_lo, wvt)
    oa = _moba(qt, bias, ka, vt)
    ob = _dilated(qkvb)
    x = _merge(x, oa, ob, row(g_pre_mix), row(g_post_mix), wu, wg,
               w_br_a.astype(BF16), w_br_b.astype(BF16), w_br_c.astype(BF16), w_out.astype(BF16),
               w_pool.astype(BF16), row(pool_scale))
    return _mlp(x, row(g_pre_mlp), row(g_post_mlp), w_ff1.astype(BF16), w_ff2.astype(BF16))


@jax.jit
def kernel(x, w_in, w_br_a, w_br_b, w_br_c, w_out, w_pool, pool_scale, g_pre_mix, g_post_mix, g_pre_mlp, g_post_mlp, w_ff1, w_ff2):
    for l in range(w_in.shape[0]):
        x = _layer(x, w_in[l], w_br_a[l], w_br_b[l], w_br_c[l], w_out[l], w_pool[l], pool_scale[l],
                   g_pre_mix[l], g_post_mix[l], g_pre_mlp[l], g_post_mlp[l], w_ff1[l], w_ff2[l])
    return x
```

```python
import jax
import jax.numpy as jnp
from jax import lax
from jax.experimental import pallas as pl
from jax.experimental.pallas import tpu as pltpu

F32 = jnp.float32
BF16 = jnp.bfloat16

D_MODEL = 1024
HEAD_DIM = 64
N_HEADS = 8
D_ATT = N_HEADS * HEAD_DIM
PAIR = 2 * HEAD_DIM
N_PAIRS = N_HEADS // 2
MOBA_BLOCK = 256
MOBA_TOPK = 3
MOBA_GROUP = 4
DENOM_ROWS = 16
LOG2_E = 1.4426950408889634
DIL_PATTERNS = ((128, 1), (512, 4), (2048, 16))
DIL_BAND = 128
DIL_TILE = 2048
POOL_WINDOWS = (2, 4, 8, 16)
POOL_WIDTH = 512
POOL_GROUP_DIM = 128
POOL_HALO = 16
D_FF = 4 * D_MODEL
RMS_EPS = 1e-6
QK_SCALE = HEAD_DIM ** -0.5
NEG = -1e30

TOK_TILE = 512
VMEM_LIMIT = 56 * 1024 * 1024

NT_DIMS = (((1,), (1,)), ((), ()))


def _dot(a, b):
    return jnp.dot(a, b, preferred_element_type=F32)


def _dot_nt(a, b):
    return lax.dot_general(a, b, NT_DIMS, preferred_element_type=F32)


def _split_bf16(v):
    hi = v.astype(BF16)
    lo = (v - hi.astype(F32)).astype(BF16)
    return hi, lo


def _rms(x, g):
    var = jnp.mean(x * x, axis=-1, keepdims=True)
    return x * lax.rsqrt(var + RMS_EPS) * g


def _in_proj_kernel(x_ref, g_ref, wk_hi_ref, wk_lo_ref, wtok_ref, wqt_hi_ref, wqt_lo_ref, wvt_ref,
                    ka_ref, qkvb_ref, qt_ref, bias_ref, vt_ref, kbar_ref):
    t = pl.program_id(1)
    blocks_per_tile = TOK_TILE // MOBA_BLOCK

    @pl.when(t == 0)
    def _():
        kbar_ref[...] = jnp.zeros_like(kbar_ref)

    h = _rms(x_ref[0], g_ref[...])
    h_hi, h_lo = _split_bf16(h)

    ka = _dot(h_hi, wk_hi_ref[...]) + _dot(h_hi, wk_lo_ref[...]) + _dot(h_lo, wk_hi_ref[...])
    ka_ref[0] = ka.astype(BF16)
    for blk in range(blocks_per_tile):
        kbar_ref[pl.ds(t * blocks_per_tile + blk, 1), :] = jnp.mean(
            ka[blk * MOBA_BLOCK:(blk + 1) * MOBA_BLOCK], axis=0, keepdims=True)

    qkvb = _dot(h_hi, wtok_ref[...])
    qkvb_ref[0, :, 0:D_ATT] = qkvb[:, 0:D_ATT] * QK_SCALE
    qkvb_ref[0, :, D_ATT:3 * D_ATT] = qkvb[:, D_ATT:3 * D_ATT]

    qt = (_dot_nt(wqt_hi_ref[...], h_hi) + _dot_nt(wqt_lo_ref[...], h_hi)
          + _dot_nt(wqt_hi_ref[...], h_lo)) * (QK_SCALE * LOG2_E)
    qt_ref[0] = qt.astype(BF16)
    vt_ref[0] = _dot_nt(wvt_ref[...], h_hi).astype(BF16)

    half = D_ATT // 2
    r = lax.broadcasted_iota(jnp.int32, (half, half), 0) // HEAD_DIM
    c = lax.broadcasted_iota(jnp.int32, (half, half), 1) // HEAD_DIM
    same_head = r == c
    n_blocks = kbar_ref.shape[0]
    blk_row = lax.broadcasted_iota(jnp.int32, (n_blocks, TOK_TILE), 0).astype(F32)
    q_blk = (t * blocks_per_tile
             + lax.broadcasted_iota(jnp.int32, (n_blocks, TOK_TILE), 1) // MOBA_BLOCK).astype(F32)
    valid = blk_row < q_blk
    for quad in range(2):
        kq = kbar_ref[:, quad * half:(quad + 1) * half]
        kb = jnp.where(same_head, jnp.concatenate([kq] * 4, axis=0), 0.0)
        kb_hi, kb_lo = _split_bf16(kb)
        q_hi, q_lo = _split_bf16(qt[quad * half:(quad + 1) * half])
        gs = _dot(kb_hi, q_hi) + _dot(kb_hi, q_lo) + _dot(kb_lo, q_hi)
        for hq in range(4):
            g = jnp.where(valid, gs[hq * n_blocks:(hq + 1) * n_blocks], -jnp.inf)
            sel = jnp.zeros(g.shape, jnp.bool_)
            for _ in range(MOBA_TOPK):
                m = jnp.max(g, axis=0, keepdims=True)
                idx = jnp.min(jnp.where(g == m, blk_row, float(n_blocks)), axis=0, keepdims=True)
                pick = blk_row == idx
                sel = jnp.logical_or(sel, pick)
                g = jnp.where(pick, -jnp.inf, g)
            head = quad * 4 + hq
            bias_ref[0, head * n_blocks:(head + 1) * n_blocks, :] = jnp.where(
                jnp.logical_and(sel, valid), 0.0, NEG)


def _in_proj(x, g, wk_hi, wk_lo, wtok, wqt_hi, wqt_lo, wvt):
    b, s, d = x.shape
    n_blocks = s // MOBA_BLOCK
    const = lambda shape: pl.BlockSpec(shape, lambda bi, ti: (0,) * len(shape))
    return pl.pallas_call(
        _in_proj_kernel,
        grid=(b, s // TOK_TILE),
        in_specs=[
            pl.BlockSpec((1, TOK_TILE, d), lambda bi, ti: (bi, ti, 0)),
            const((1, d)),
            const((d, D_ATT)), const((d, D_ATT)), const((d, 3 * D_ATT)),
            const((D_ATT, d)), const((D_ATT, d)), const((D_ATT, d)),
        ],
        out_specs=[
            pl.BlockSpec((1, TOK_TILE, D_ATT), lambda bi, ti: (bi, ti, 0)),
            pl.BlockSpec((1, TOK_TILE, 3 * D_ATT), lambda bi, ti: (bi, ti, 0)),
            pl.BlockSpec((1, D_ATT, TOK_TILE), lambda bi, ti: (bi, 0, ti)),
            pl.BlockSpec((1, N_HEADS * n_blocks, TOK_TILE), lambda bi, ti: (bi, 0, ti)),
            pl.BlockSpec((1, D_ATT, TOK_TILE), lambda bi, ti: (bi, 0, ti)),
        ],
        out_shape=[
            jax.ShapeDtypeStruct((b, s, D_ATT), BF16),
            jax.ShapeDtypeStruct((b, s, 3 * D_ATT), F32),
            jax.ShapeDtypeStruct((b, D_ATT, s), BF16),
            jax.ShapeDtypeStruct((b, N_HEADS * n_blocks, s), F32),
            jax.ShapeDtypeStruct((b, D_ATT, s), BF16),
        ],
        scratch_shapes=[pltpu.VMEM((n_blocks, D_ATT), F32)],
        compiler_params=pltpu.CompilerParams(
            dimension_semantics=("arbitrary", "arbitrary"), vmem_limit_bytes=VMEM_LIMIT),
        name="in_proj",
    )(x, g, wk_hi, wk_lo, wtok, wqt_hi, wqt_lo, wvt)


def _moba_kernel(qt_ref, bias_ref, k_ref, vt_ref, o_ref, s_sc, p_sc):
    i = pl.program_id(2)
    n_blocks = bias_ref.shape[1] // 2
    n_groups = n_blocks // MOBA_GROUP
    group_keys = MOBA_GROUP * MOBA_BLOCK
    own_g = i // MOBA_GROUP
    qt = qt_ref[0]
    zeros = jnp.zeros((HEAD_DIM, MOBA_BLOCK), BF16)
    qt_heads = (jnp.concatenate([qt[:HEAD_DIM], zeros], axis=0),
                jnp.concatenate([zeros, qt[HEAD_DIM:]], axis=0))
    key_pos = lax.broadcasted_iota(jnp.int32, (MOBA_BLOCK, MOBA_BLOCK), 0)
    qry_pos = lax.broadcasted_iota(jnp.int32, (MOBA_BLOCK, MOBA_BLOCK), 1)
    future = key_pos > qry_pos
    ones_rows = jnp.ones((DENOM_ROWS, group_keys), BF16)

    def scores(g, slot):
        k_g = k_ref[0, pl.ds(pl.multiple_of(g * group_keys, group_keys), group_keys), :]
        for hd in range(2):
            s_sc[slot, hd] = _dot(k_g, qt_heads[hd])

    def softmax(g, slot, m_pair, own_group=False, penalty=None):
        alphas, m_news = [], []
        for hd in range(2):
            blocks, gates = [], []
            for r in range(MOBA_GROUP):
                s_r = s_sc[slot, hd, r * MOBA_BLOCK:(r + 1) * MOBA_BLOCK, :]
                gate = bias_ref[0, pl.ds(hd * n_blocks + g * MOBA_GROUP + r, 1), :]
                if own_group:
                    own = g * MOBA_GROUP + r == i
                    s_r = s_r + jnp.where(jnp.logical_and(own, future), NEG, 0.0)
                    gate = jnp.where(own, 0.0, gate)
                if penalty is not None:
                    gate = gate + penalty
                blocks.append(s_r)
                gates.append(gate)
            m_new = m_pair[hd]
            for s_r, gate in zip(blocks, gates):
                m_new = jnp.maximum(m_new, jnp.max(s_r, axis=0, keepdims=True) + gate)
            alphas.append(jnp.exp2(m_pair[hd] - m_new))
            m_news.append(m_new)
            for r, (s_r, gate) in enumerate(zip(blocks, gates)):
                p_sc[slot, hd, r * MOBA_BLOCK:(r + 1) * MOBA_BLOCK, :] = jnp.exp2(s_r - (m_new - gate)).astype(BF16)
        return tuple(alphas), tuple(m_news)

    def accum(g, slot, alphas, accs):
        vt_g = vt_ref[0, :, pl.ds(pl.multiple_of(g * group_keys, group_keys), group_keys)]
        out = []
        for hd in range(2):
            lhs = jnp.concatenate([vt_g[hd * HEAD_DIM:(hd + 1) * HEAD_DIM], ones_rows], axis=0)
            out.append(alphas[hd] * accs[hd] + _dot(lhs, p_sc[slot, hd]))
        return tuple(out)

    m0 = (jnp.full((1, MOBA_BLOCK), NEG, F32),) * 2
    acc0 = (jnp.zeros((HEAD_DIM + DENOM_ROWS, MOBA_BLOCK), F32),) * 2
    scores(own_g, 0)
    alpha0, m1 = softmax(own_g, 0, m0, own_group=True)
    scores(0, 1)

    def two_steps(d, carry):
        alpha_prev, m, acc = carry
        g_a = 2 * d
        g_b = jnp.minimum(g_a + 1, n_groups - 1)
        g_prev = jnp.where(d == 0, own_g, g_a - 1)
        scores(g_b, 0)
        alpha_a, m = softmax(g_a, 1, m)
        acc = accum(g_prev, 0, alpha_prev, acc)
        scores(jnp.minimum(g_a + 2, n_groups - 1), 1)
        padded = g_a + 1 >= own_g
        alpha_b, m = softmax(g_b, 0, m, penalty=jnp.where(padded, NEG, 0.0))
        acc = accum(g_a, 1, alpha_a, acc)
        return alpha_b, m, acc

    n_double = (own_g + 1) // 2
    alpha_l, _, acc = lax.fori_loop(0, n_double, two_steps, (alpha0, m1, acc0))
    g_last = jnp.where(n_double == 0, own_g, jnp.minimum(2 * n_double - 1, n_groups - 1))
    acc = accum(g_last, 0, alpha_l, acc)
    ot = jnp.concatenate([a[:HEAD_DIM] * (1.0 / a[HEAD_DIM:HEAD_DIM + 1]) for a in acc], axis=0)
    o_ref[0] = ot.T.astype(BF16)


def _moba(qt, bias, ka, vt):
    b, s, _ = ka.shape
    n_blocks = s // MOBA_BLOCK
    return pl.pallas_call(
        _moba_kernel,
        grid=(b, N_PAIRS, n_blocks),
        in_specs=[
            pl.BlockSpec((1, PAIR, MOBA_BLOCK), lambda bi, p, i: (bi, p, i)),
            pl.BlockSpec((1, 2 * n_blocks, MOBA_BLOCK), lambda bi, p, i: (bi, p, i)),
            pl.BlockSpec((1, s, PAIR), lambda bi, p, i: (bi, 0, p)),
            pl.BlockSpec((1, PAIR, s), lambda bi, p, i: (bi, p, 0)),
        ],
        out_specs=pl.BlockSpec((1, MOBA_BLOCK, PAIR), lambda bi, p, i: (bi, i, p)),
        out_shape=jax.ShapeDtypeStruct((b, s, D_ATT), BF16),
        scratch_shapes=[pltpu.VMEM((2, 2, MOBA_GROUP * MOBA_BLOCK, MOBA_BLOCK), F32),
                        pltpu.VMEM((2, 2, MOBA_GROUP * MOBA_BLOCK, MOBA_BLOCK), BF16)],
        compiler_params=pltpu.CompilerParams(
            dimension_semantics=("arbitrary", "arbitrary", "arbitrary"), vmem_limit_bytes=VMEM_LIMIT),
        name="moba",
    )(qt, bias, ka, vt)


def _dilated_kernel(q_ref, kc_ref, kp_ref, vc_ref, vp_ref, o_ref, o_sc, lse_sc):
    first_tile = pl.program_id(2) == 0
    n_tiles = DIL_TILE // DIL_BAND
    head_a = lax.broadcasted_iota(jnp.int32, (DIL_TILE, PAIR), 1) < HEAD_DIM
    qi = lax.broadcasted_iota(jnp.int32, (DIL_BAND, 2 * DIL_BAND), 0)
    kj = lax.broadcasted_iota(jnp.int32, (DIL_BAND, 2 * DIL_BAND), 1)
    band = jnp.where(jnp.logical_and(kj >= qi, kj <= qi + DIL_BAND), 0.0, NEG)
    band_start = jnp.where(first_tile, jnp.where(kj >= DIL_BAND, band, NEG), band)

    for pat, (window, dil) in enumerate(DIL_PATTERNS):
        rows = DIL_TILE // dil
        n_q = rows // DIL_BAND
        strided = lambda ref, start, size: ref[0, pl.ds(start, size, stride=dil), :]
        q_parts, k_tiles, v_tiles, masks = [], [], [], []
        for r in range(dil):
            q_parts.append(strided(q_ref, r, rows))
            prev_start = DIL_TILE - DIL_BAND * dil + r
            k_ext = jnp.concatenate([strided(kp_ref, prev_start, DIL_BAND), strided(kc_ref, r, rows)],
                                    axis=0).astype(BF16)
            v_ext = jnp.concatenate([strided(vp_ref, prev_start, DIL_BAND), strided(vc_ref, r, rows)],
                                    axis=0).astype(BF16)
            for qb in range(n_q):
                k_tiles.append(k_ext[qb * DIL_BAND:(qb + 2) * DIL_BAND])
                v_tiles.append(v_ext[qb * DIL_BAND:(qb + 2) * DIL_BAND])
                masks.append(band_start if qb == 0 else band)
        q_all = jnp.concatenate(q_parts, axis=0)
        mask_all = jnp.concatenate(masks, axis=0)
        outs, lses = [], []
        for hd in range(2):
            qh = jnp.where(head_a if hd == 0 else jnp.logical_not(head_a), q_all, 0.0).astype(BF16)
            s = jnp.concatenate(
                [_dot_nt(qh[n * DIL_BAND:(n + 1) * DIL_BAND], k_tiles[n]) for n in range(n_tiles)],
                axis=0) + mask_all
            m = jnp.max(s, axis=-1, keepdims=True)
            p = jnp.exp(s - m)
            l = jnp.sum(p, axis=-1, keepdims=True)
            pb = p.astype(BF16)
            o = jnp.concatenate(
                [_dot(pb[n * DIL_BAND:(n + 1) * DIL_BAND], v_tiles[n]) for n in range(n_tiles)], axis=0)
            outs.append(o / l)
            lses.append(m + jnp.log(l))
        o_pair = jnp.where(head_a, outs[0], outs[1])
        lse_pair = jnp.where(head_a, lses[0], lses[1])
        for r in range(dil):
            o_sc[pat, pl.ds(r, rows, stride=dil), :] = o_pair[r * rows:(r + 1) * rows]
            lse_sc[pat, pl.ds(r, rows, stride=dil), :] = lse_pair[r * rows:(r + 1) * rows]

    lse = lse_sc[...]
    w = jnp.exp(lse - jnp.max(lse, axis=0, keepdims=True))
    o_ref[0] = (jnp.sum(w * o_sc[...], axis=0) / jnp.sum(w, axis=0)).astype(BF16)


def _dilated(qkvb):
    b, s, _ = qkvb.shape
    tile_spec = lambda col0, prev: pl.BlockSpec(
        (1, DIL_TILE, PAIR),
        (lambda bi, p, t: (bi, jnp.maximum(t - 1, 0), col0 + p)) if prev else (lambda bi, p, t: (bi, t, col0 + p)))
    return pl.pallas_call(
        _dilated_kernel,
        grid=(b, N_PAIRS, s // DIL_TILE),
        in_specs=[tile_spec(0, False),
                  tile_spec(N_PAIRS, False), tile_spec(N_PAIRS, True),
                  tile_spec(2 * N_PAIRS, False), tile_spec(2 * N_PAIRS, True)],
        out_specs=pl.BlockSpec((1, DIL_TILE, PAIR), lambda bi, p, t: (bi, t, p)),
        out_shape=jax.ShapeDtypeStruct((b, s, D_ATT), BF16),
        scratch_shapes=[pltpu.VMEM((len(DIL_PATTERNS), DIL_TILE, PAIR), F32),
                        pltpu.VMEM((len(DIL_PATTERNS), DIL_TILE, PAIR), F32)],
        compiler_params=pltpu.CompilerParams(
            dimension_semantics=("arbitrary", "arbitrary", "arbitrary"), vmem_limit_bytes=VMEM_LIMIT),
        name="dilated",
    )(qkvb, qkvb, qkvb, qkvb, qkvb)


def _merge_kernel(x_ref, xh_ref, oa_ref, ob_ref, gpre_ref, gpost_ref, wu_ref, wg_ref, wa_ref, wb_ref, wc_ref,
                  wout_ref, wpool_ref, pscale_ref, o_ref):
    t = pl.program_id(1)
    x = x_ref[0]
    h = _rms(x, gpre_ref[...]).astype(BF16)
    hh = _rms(xh_ref[0], gpre_ref[...]).astype(BF16)
    u = _dot(h, wu_ref[...])
    uh = jnp.where(t == 0, 0.0, _dot(hh, wu_ref[...]))
    u_ext = jnp.concatenate([uh, u], axis=0)
    pos = (t * TOK_TILE + lax.broadcasted_iota(jnp.int32, (TOK_TILE, POOL_GROUP_DIM), 0) + 1).astype(F32)
    oc = []
    for g, w in enumerate(POOL_WINDOWS):
        cols = slice(g * POOL_GROUP_DIM, (g + 1) * POOL_GROUP_DIM)
        acc = u_ext[:, cols]
        span = 1
        while span < w:
            acc = acc + pltpu.roll(acc, span, 0)
            span *= 2
        pooled = acc[POOL_HALO:] / jnp.minimum(pos, float(w)) - u[:, cols]
        oc.append(_dot(pooled.astype(BF16), wpool_ref[g]))
    oc = (jnp.concatenate(oc, axis=1) * pscale_ref[...]).astype(BF16)

    merged = None
    for br, (src, w_ref) in enumerate(((oa_ref[0], wa_ref), (ob_ref[0], wb_ref), (oc, wc_ref))):
        gz = _dot(h, wg_ref[:, br * D_MODEL:(br + 1) * D_MODEL])
        term = (1.0 / (1.0 + jnp.exp(-gz))) * _dot(src, w_ref[...])
        merged = term if merged is None else merged + term
    y = _dot(merged.astype(BF16), wout_ref[...])
    o_ref[0] = x + _rms(y, gpost_ref[...])


def _merge(x, oa, ob, gpre, gpost, wu, wg, wa, wb, wc, wout, wpool, pscale):
    b, s, d = x.shape
    const = lambda shape: pl.BlockSpec(shape, lambda bi, ti: (0,) * len(shape))
    halo_blocks = TOK_TILE // POOL_HALO
    return pl.pallas_call(
        _merge_kernel,
        grid=(b, s // TOK_TILE),
        in_specs=[
            pl.BlockSpec((1, TOK_TILE, d), lambda bi, ti: (bi, ti, 0)),
            pl.BlockSpec((1, POOL_HALO, d), lambda bi, ti: (bi, jnp.maximum(ti * halo_blocks - 1, 0), 0)),
            pl.BlockSpec((1, TOK_TILE, D_ATT), lambda bi, ti: (bi, ti, 0)),
            pl.BlockSpec((1, TOK_TILE, D_ATT), lambda bi, ti: (bi, ti, 0)),
            const((1, d)), const((1, d)),
            const((d, POOL_WIDTH)), const((d, 3 * d)),
            const((D_ATT, d)), const((D_ATT, d)), const((POOL_WIDTH, d)),
            const((d, d)), const((len(POOL_WINDOWS), POOL_GROUP_DIM, POOL_GROUP_DIM)), const((1, POOL_WIDTH)),
        ],
        out_specs=pl.BlockSpec((1, TOK_TILE, d), lambda bi, ti: (bi, ti, 0)),
        out_shape=jax.ShapeDtypeStruct((b, s, d), F32),
        compiler_params=pltpu.CompilerParams(
            dimension_semantics=("arbitrary", "arbitrary"), vmem_limit_bytes=VMEM_LIMIT),
        name="merge",
    )(x, x, oa, ob, gpre, gpost, wu, wg, wa, wb, wc, wout, wpool, pscale)


def _mlp_kernel(x_ref, gpre_ref, gpost_ref, w1_ref, w2_ref, o_ref):
    x = x_ref[0]
    h = _rms(x, gpre_ref[...]).astype(BF16)
    y = None
    for c in range(D_FF // D_MODEL):
        cols = slice(c * D_MODEL, (c + 1) * D_MODEL)
        a = jnp.maximum(_dot(h, w1_ref[:, cols]), 0.0)
        part = _dot((a * a).astype(BF16), w2_ref[cols, :])
        y = part if y is None else y + part
    o_ref[0] = x + _rms(y, gpost_ref[...])


def _mlp(x, gpre, gpost, w1, w2):
    b, s, d = x.shape
    const = lambda shape: pl.BlockSpec(shape, lambda bi, ti: (0,) * len(shape))
    return pl.pallas_call(
        _mlp_kernel,
        grid=(b, s // TOK_TILE),
        in_specs=[pl.BlockSpec((1, TOK_TILE, d), lambda bi, ti: (bi, ti, 0)),
                  const((1, d)), const((1, d)), const((d, D_FF)), const((D_FF, d))],
        out_specs=pl.BlockSpec((1, TOK_TILE, d), lambda bi, ti: (bi, ti, 0)),
        out_shape=jax.ShapeDtypeStruct((b, s, d), F32),
        compiler_params=pltpu.CompilerParams(
            dimension_semantics=("arbitrary", "arbitrary"), vmem_limit_bytes=VMEM_LIMIT),
        name="mlp",
    )(x, gpre, gpost, w1, w2)


def _layer(x, w_in, w_br_a, w_br_b, w_br_c, w_out, w_pool, pool_scale,
           g_pre_mix, g_post_mix, g_pre_mlp, g_post_mlp, w_ff1, w_ff2):
    col = lambda k: slice(k * D_ATT, (k + 1) * D_ATT)
    w_qa, w_ka, w_va = w_in[:, col(0)], w_in[:, col(1)], w_in[:, col(2)]
    wk_hi, wk_lo = _split_bf16(w_ka)
    wqt_hi, wqt_lo = _split_bf16(w_qa.T)
    wvt = w_va.T.astype(BF16)
    wtok = w_in[:, 3 * D_ATT:6 * D_ATT].astype(BF16)
    wu = w_in[:, 6 * D_ATT:6 * D_ATT + POOL_WIDTH].astype(BF16)
    wg = w_in[:, 6 * D_ATT + POOL_WIDTH:].astype(BF16)
    row = lambda v: v.reshape(1, -1)

    ka, qkvb, qt, bias, vt = _in_proj(x, row(g_pre_mix), wk_hi, wk_lo, wtok, wqt_hi, wqt_lo, wvt)
    oa = _moba(qt, bias, ka, vt)
    ob = _dilated(qkvb)
    x = _merge(x, oa, ob, row(g_pre_mix), row(g_post_mix), wu, wg,
               w_br_a.astype(BF16), w_br_b.astype(BF16), w_br_c.astype(BF16), w_out.astype(BF16),
               w_pool.astype(BF16), row(pool_scale))
    return _mlp(x, row(g_pre_mlp), row(g_post_mlp), w_ff1.astype(BF16), w_ff2.astype(BF16))


@jax.jit
def kernel(x, w_in, w_br_a, w_br_b, w_br_c, w_out, w_pool, pool_scale, g_pre_mix, g_post_mix, g_pre_mlp, g_post_mlp, w_ff1, w_ff2):
    for l in range(w_in.shape[0]):
        x = _layer(x, w_in[l], w_br_a[l], w_br_b[l], w_br_c[l], w_out[l], w_pool[l], pool_scale[l],
                   g_pre_mix[l], g_post_mix[l], g_pre_mlp[l], g_post_mlp[l], w_ff1[l], w_ff2[l])
    return x
```

```python
import jax
import jax.numpy as jnp
from jax import lax
from jax.experimental import pallas as pl
from jax.experimental.pallas import tpu as pltpu

F32 = jnp.float32
BF16 = jnp.bfloat16

D_MODEL = 1024
HEAD_DIM = 64
N_HEADS = 8
D_ATT = N_HEADS * HEAD_DIM
PAIR = 2 * HEAD_DIM
N_PAIRS = N_HEADS // 2
MOBA_BLOCK = 256
MOBA_TOPK = 3
MOBA_GROUP = 4
MOBA_QTILE = 512
DENOM_ROWS = 16
LOG2_E = 1.4426950408889634
DIL_PATTERNS = ((128, 1), (512, 4), (2048, 16))
DIL_BAND = 128
DIL_TILE = 2048
POOL_WINDOWS = (2, 4, 8, 16)
POOL_WIDTH = 512
POOL_GROUP_DIM = 128
POOL_HALO = 16
D_FF = 4 * D_MODEL
RMS_EPS = 1e-6
QK_SCALE = HEAD_DIM ** -0.5
NEG = -1e30

TOK_TILE = 512
VMEM_LIMIT = 56 * 1024 * 1024

NT_DIMS = (((1,), (1,)), ((), ()))


def _dot(a, b):
    return jnp.dot(a, b, preferred_element_type=F32)


def _dot_nt(a, b):
    return lax.dot_general(a, b, NT_DIMS, preferred_element_type=F32)


def _split_bf16(v):
    hi = v.astype(BF16)
    lo = (v - hi.astype(F32)).astype(BF16)
    return hi, lo


def _rms(x, g):
    var = jnp.mean(x * x, axis=-1, keepdims=True)
    return x * lax.rsqrt(var + RMS_EPS) * g


def _in_proj_kernel(x_ref, g_ref, wk_hi_ref, wk_lo_ref, wtok_ref, wqt_hi_ref, wqt_lo_ref, wvt_ref,
                    ka_ref, qkvb_ref, qt_ref, bias_ref, vt_ref, kbar_ref):
    t = pl.program_id(1)
    blocks_per_tile = TOK_TILE // MOBA_BLOCK

    @pl.when(t == 0)
    def _():
        kbar_ref[...] = jnp.zeros_like(kbar_ref)

    h = _rms(x_ref[0], g_ref[...])
    h_hi, h_lo = _split_bf16(h)

    ka = _dot(h_hi, wk_hi_ref[...]) + _dot(h_hi, wk_lo_ref[...]) + _dot(h_lo, wk_hi_ref[...])
    ka_ref[0] = ka.astype(BF16)
    for blk in range(blocks_per_tile):
        kbar_ref[pl.ds(t * blocks_per_tile + blk, 1), :] = jnp.mean(
            ka[blk * MOBA_BLOCK:(blk + 1) * MOBA_BLOCK], axis=0, keepdims=True)

    qkvb = _dot(h_hi, wtok_ref[...])
    qkvb_ref[0, :, 0:D_ATT] = qkvb[:, 0:D_ATT] * QK_SCALE
    qkvb_ref[0, :, D_ATT:3 * D_ATT] = qkvb[:, D_ATT:3 * D_ATT]

    qt = (_dot_nt(wqt_hi_ref[...], h_hi) + _dot_nt(wqt_lo_ref[...], h_hi)
          + _dot_nt(wqt_hi_ref[...], h_lo)) * (QK_SCALE * LOG2_E)
    qt_ref[0] = qt.astype(BF16)
    vt_ref[0] = _dot_nt(wvt_ref[...], h_hi).astype(BF16)

    half = D_ATT // 2
    r = lax.broadcasted_iota(jnp.int32, (half, half), 0) // HEAD_DIM
    c = lax.broadcasted_iota(jnp.int32, (half, half), 1) // HEAD_DIM
    same_head = r == c
    n_blocks = kbar_ref.shape[0]
    blk_row = lax.broadcasted_iota(jnp.int32, (n_blocks, TOK_TILE), 0).astype(F32)
    q_blk = (t * blocks_per_tile
             + lax.broadcasted_iota(jnp.int32, (n_blocks, TOK_TILE), 1) // MOBA_BLOCK).astype(F32)
    valid = blk_row < q_blk
    for quad in range(2):
        kq = kbar_ref[:, quad * half:(quad + 1) * half]
        kb = jnp.where(same_head, jnp.concatenate([kq] * 4, axis=0), 0.0)
        kb_hi, kb_lo = _split_bf16(kb)
        q_hi, q_lo = _split_bf16(qt[quad * half:(quad + 1) * half])
        gs = _dot(kb_hi, q_hi) + _dot(kb_hi, q_lo) + _dot(kb_lo, q_hi)
        for hq in range(4):
            g = jnp.where(valid, gs[hq * n_blocks:(hq + 1) * n_blocks], -jnp.inf)
            sel = jnp.zeros(g.shape, jnp.bool_)
            for _ in range(MOBA_TOPK):
                m = jnp.max(g, axis=0, keepdims=True)
                idx = jnp.min(jnp.where(g == m, blk_row, float(n_blocks)), axis=0, keepdims=True)
                pick = blk_row == idx
                sel = jnp.logical_or(sel, pick)
                g = jnp.where(pick, -jnp.inf, g)
            head = quad * 4 + hq
            bias_ref[0, head * n_blocks:(head + 1) * n_blocks, :] = jnp.where(
                jnp.logical_and(sel, valid), 0.0, NEG)


def _in_proj(x, g, wk_hi, wk_lo, wtok, wqt_hi, wqt_lo, wvt):
    b, s, d = x.shape
    n_blocks = s // MOBA_BLOCK
    const = lambda shape: pl.BlockSpec(shape, lambda bi, ti: (0,) * len(shape))
    return pl.pallas_call(
        _in_proj_kernel,
        grid=(b, s // TOK_TILE),
        in_specs=[
            pl.BlockSpec((1, TOK_TILE, d), lambda bi, ti: (bi, ti, 0)),
            const((1, d)),
            const((d, D_ATT)), const((d, D_ATT)), const((d, 3 * D_ATT)),
            const((D_ATT, d)), const((D_ATT, d)), const((D_ATT, d)),
        ],
        out_specs=[
            pl.BlockSpec((1, TOK_TILE, D_ATT), lambda bi, ti: (bi, ti, 0)),
            pl.BlockSpec((1, TOK_TILE, 3 * D_ATT), lambda bi, ti: (bi, ti, 0)),
            pl.BlockSpec((1, D_ATT, TOK_TILE), lambda bi, ti: (bi, 0, ti)),
            pl.BlockSpec((1, N_HEADS * n_blocks, TOK_TILE), lambda bi, ti: (bi, 0, ti)),
            pl.BlockSpec((1, D_ATT, TOK_TILE), lambda bi, ti: (bi, 0, ti)),
        ],
        out_shape=[
            jax.ShapeDtypeStruct((b, s, D_ATT), BF16),
            jax.ShapeDtypeStruct((b, s, 3 * D_ATT), F32),
            jax.ShapeDtypeStruct((b, D_ATT, s), BF16),
            jax.ShapeDtypeStruct((b, N_HEADS * n_blocks, s), F32),
            jax.ShapeDtypeStruct((b, D_ATT, s), BF16),
        ],
        scratch_shapes=[pltpu.VMEM((n_blocks, D_ATT), F32)],
        compiler_params=pltpu.CompilerParams(
            dimension_semantics=("arbitrary", "arbitrary"), vmem_limit_bytes=VMEM_LIMIT),
        name="in_proj",
    )(x, g, wk_hi, wk_lo, wtok, wqt_hi, wqt_lo, wvt)


def _moba_kernel(qt_ref, bias_ref, k_ref, vt_ref, o_ref, s_sc, p_sc):
    n_blocks = bias_ref.shape[1] // 2
    n_groups = n_blocks // MOBA_GROUP
    group_keys = MOBA_GROUP * MOBA_BLOCK
    first_blk = pl.program_id(2) * (MOBA_QTILE // MOBA_BLOCK)
    own_g = first_blk // MOBA_GROUP
    qt = qt_ref[0]
    zeros = jnp.zeros((HEAD_DIM, MOBA_QTILE), BF16)
    qt_heads = (jnp.concatenate([qt[:HEAD_DIM], zeros], axis=0),
                jnp.concatenate([zeros, qt[HEAD_DIM:]], axis=0))
    key_pos = lax.broadcasted_iota(jnp.int32, (MOBA_BLOCK, MOBA_QTILE), 0)
    qry_pos = lax.broadcasted_iota(jnp.int32, (MOBA_BLOCK, MOBA_QTILE), 1)
    future = key_pos > qry_pos % MOBA_BLOCK
    qry_blk = qry_pos // MOBA_BLOCK
    qry_blk_row = qry_blk[0:1]
    ones_rows = jnp.ones((DENOM_ROWS, group_keys), BF16)

    def scores(g, slot):
        k_g = k_ref[0, pl.ds(pl.multiple_of(g * group_keys, group_keys), group_keys), :]
        for hd in range(2):
            s_sc[slot, hd] = _dot(k_g, qt_heads[hd])

    def softmax(g, slot, m_pair, own_group=False, penalty=None):
        alphas, m_news = [], []
        for hd in range(2):
            blocks, gates = [], []
            for r in range(MOBA_GROUP):
                s_r = s_sc[slot, hd, r * MOBA_BLOCK:(r + 1) * MOBA_BLOCK, :]
                gate = bias_ref[0, pl.ds(hd * n_blocks + g * MOBA_GROUP + r, 1), :]
                if own_group:
                    rel = g * MOBA_GROUP + r - first_blk
                    s_r = s_r + jnp.where(jnp.logical_and(qry_blk == rel, future), NEG, 0.0)
                    gate = jnp.where(qry_blk_row == rel, 0.0, gate)
                if penalty is not None:
                    gate = gate + penalty
                blocks.append(s_r)
                gates.append(gate)
            m_new = m_pair[hd]
            for s_r, gate in zip(blocks, gates):
                m_new = jnp.maximum(m_new, jnp.max(s_r, axis=0, keepdims=True) + gate)
            alphas.append(jnp.exp2(m_pair[hd] - m_new))
            m_news.append(m_new)
            for r, (s_r, gate) in enumerate(zip(blocks, gates)):
                p_sc[slot, hd, r * MOBA_BLOCK:(r + 1) * MOBA_BLOCK, :] = jnp.exp2(s_r - (m_new - gate)).astype(BF16)
        return tuple(alphas), tuple(m_news)

    def accum(g, slot, alphas, accs):
        vt_g = vt_ref[0, :, pl.ds(pl.multiple_of(g * group_keys, group_keys), group_keys)]
        out = []
        for hd in range(2):
            lhs = jnp.concatenate([vt_g[hd * HEAD_DIM:(hd + 1) * HEAD_DIM], ones_rows], axis=0)
            out.append(alphas[hd] * accs[hd] + _dot(lhs, p_sc[slot, hd]))
        return tuple(out)

    m0 = (jnp.full((1, MOBA_QTILE), NEG, F32),) * 2
    acc0 = (jnp.zeros((HEAD_DIM + DENOM_ROWS, MOBA_QTILE), F32),) * 2
    scores(own_g, 0)
    alpha0, m1 = softmax(own_g, 0, m0, own_group=True)
    scores(0, 1)

    def two_steps(d, carry):
        alpha_prev, m, acc = carry
        g_a = 2 * d
        g_b = jnp.minimum(g_a + 1, n_groups - 1)
        g_prev = jnp.where(d == 0, own_g, g_a - 1)
        scores(g_b, 0)
        alpha_a, m = softmax(g_a, 1, m)
        acc = accum(g_prev, 0, alpha_prev, acc)
        scores(jnp.minimum(g_a + 2, n_groups - 1), 1)
        padded = g_a + 1 >= own_g
        alpha_b, m = softmax(g_b, 0, m, penalty=jnp.where(padded, NEG, 0.0))
        acc = accum(g_a, 1, alpha_a, acc)
        return alpha_b, m, acc

    n_double = (own_g + 1) // 2
    alpha_l, _, acc = lax.fori_loop(0, n_double, two_steps, (alpha0, m1, acc0))
    g_last = jnp.where(n_double == 0, own_g, jnp.minimum(2 * n_double - 1, n_groups - 1))
    acc = accum(g_last, 0, alpha_l, acc)
    ot = jnp.concatenate([a[:HEAD_DIM] * (1.0 / a[HEAD_DIM:HEAD_DIM + 1]) for a in acc], axis=0)
    o_ref[0] = ot.T.astype(BF16)


def _moba(qt, bias, ka, vt):
    b, s, _ = ka.shape
    n_blocks = s // MOBA_BLOCK
    return pl.pallas_call(
        _moba_kernel,
        grid=(b, N_PAIRS, s // MOBA_QTILE),
        in_specs=[
            pl.BlockSpec((1, PAIR, MOBA_QTILE), lambda bi, p, i: (bi, p, i)),
            pl.BlockSpec((1, 2 * n_blocks, MOBA_QTILE), lambda bi, p, i: (bi, p, i)),
            pl.BlockSpec((1, s, PAIR), lambda bi, p, i: (bi, 0, p)),
            pl.BlockSpec((1, PAIR, s), lambda bi, p, i: (bi, p, 0)),
        ],
        out_specs=pl.BlockSpec((1, MOBA_QTILE, PAIR), lambda bi, p, i: (bi, i, p)),
        out_shape=jax.ShapeDtypeStruct((b, s, D_ATT), BF16),
        scratch_shapes=[pltpu.VMEM((2, 2, MOBA_GROUP * MOBA_BLOCK, MOBA_QTILE), F32),
                        pltpu.VMEM((2, 2, MOBA_GROUP * MOBA_BLOCK, MOBA_QTILE), BF16)],
        compiler_params=pltpu.CompilerParams(
            dimension_semantics=("arbitrary", "arbitrary", "arbitrary"), vmem_limit_bytes=VMEM_LIMIT),
        name="moba",
    )(qt, bias, ka, vt)


def _dilated_kernel(q_ref, kc_ref, kp_ref, vc_ref, vp_ref, o_ref, o_sc, lse_sc):
    first_tile = pl.program_id(2) == 0
    n_tiles = DIL_TILE // DIL_BAND
    head_a = lax.broadcasted_iota(jnp.int32, (DIL_TILE, PAIR), 1) < HEAD_DIM
    qi = lax.broadcasted_iota(jnp.int32, (DIL_BAND, 2 * DIL_BAND), 0)
    kj = lax.broadcasted_iota(jnp.int32, (DIL_BAND, 2 * DIL_BAND), 1)
    band = jnp.where(jnp.logical_and(kj >= qi, kj <= qi + DIL_BAND), 0.0, NEG)
    band_start = jnp.where(first_tile, jnp.where(kj >= DIL_BAND, band, NEG), band)

    for pat, (window, dil) in enumerate(DIL_PATTERNS):
        rows = DIL_TILE // dil
        n_q = rows // DIL_BAND
        strided = lambda ref, start, size: ref[0, pl.ds(start, size, stride=dil), :]
        q_parts, k_tiles, v_tiles, masks = [], [], [], []
        for r in range(dil):
            q_parts.append(strided(q_ref, r, rows))
            prev_start = DIL_TILE - DIL_BAND * dil + r
            k_ext = jnp.concatenate([strided(kp_ref, prev_start, DIL_BAND), strided(kc_ref, r, rows)],
                                    axis=0).astype(BF16)
            v_ext = jnp.concatenate([strided(vp_ref, prev_start, DIL_BAND), strided(vc_ref, r, rows)],
                                    axis=0).astype(BF16)
            for qb in range(n_q):
                k_tiles.append(k_ext[qb * DIL_BAND:(qb + 2) * DIL_BAND])
                v_tiles.append(v_ext[qb * DIL_BAND:(qb + 2) * DIL_BAND])
                masks.append(band_start if qb == 0 else band)
        q_all = jnp.concatenate(q_parts, axis=0)
        mask_all = jnp.concatenate(masks, axis=0)
        outs, lses = [], []
        for hd in range(2):
            qh = jnp.where(head_a if hd == 0 else jnp.logical_not(head_a), q_all, 0.0).astype(BF16)
            s = jnp.concatenate(
                [_dot_nt(qh[n * DIL_BAND:(n + 1) * DIL_BAND], k_tiles[n]) for n in range(n_tiles)],
                axis=0) + mask_all
            m = jnp.max(s, axis=-1, keepdims=True)
            p = jnp.exp(s - m)
            l = jnp.sum(p, axis=-1, keepdims=True)
            pb = p.astype(BF16)
            o = jnp.concatenate(
                [_dot(pb[n * DIL_BAND:(n + 1) * DIL_BAND], v_tiles[n]) for n in range(n_tiles)], axis=0)
            outs.append(o / l)
            lses.append(m + jnp.log(l))
        o_pair = jnp.where(head_a, outs[0], outs[1])
        lse_pair = jnp.where(head_a, lses[0], lses[1])
        for r in range(dil):
            o_sc[pat, pl.ds(r, rows, stride=dil), :] = o_pair[r * rows:(r + 1) * rows]
            lse_sc[pat, pl.ds(r, rows, stride=dil), :] = lse_pair[r * rows:(r + 1) * rows]

    lse = lse_sc[...]
    w = jnp.exp(lse - jnp.max(lse, axis=0, keepdims=True))
    o_ref[0] = (jnp.sum(w * o_sc[...], axis=0) / jnp.sum(w, axis=0)).astype(BF16)


def _dilated(qkvb):
    b, s, _ = qkvb.shape
    tile_spec = lambda col0, prev: pl.BlockSpec(
        (1, DIL_TILE, PAIR),
        (lambda bi, p, t: (bi, jnp.maximum(t - 1, 0), col0 + p)) if prev else (lambda bi, p, t: (bi, t, col0 + p)))
    return pl.pallas_call(
        _dilated_kernel,
        grid=(b, N_PAIRS, s // DIL_TILE),
        in_specs=[tile_spec(0, False),
                  tile_spec(N_PAIRS, False), tile_spec(N_PAIRS, True),
                  tile_spec(2 * N_PAIRS, False), tile_spec(2 * N_PAIRS, True)],
        out_specs=pl.BlockSpec((1, DIL_TILE, PAIR), lambda bi, p, t: (bi, t, p)),
        out_shape=jax.ShapeDtypeStruct((b, s, D_ATT), BF16),
        scratch_shapes=[pltpu.VMEM((len(DIL_PATTERNS), DIL_TILE, PAIR), F32),
                        pltpu.VMEM((len(DIL_PATTERNS), DIL_TILE, PAIR), F32)],
        compiler_params=pltpu.CompilerParams(
            dimension_semantics=("arbitrary", "arbitrary", "arbitrary"), vmem_limit_bytes=VMEM_LIMIT),
        name="dilated",
    )(qkvb, qkvb, qkvb, qkvb, qkvb)


def _merge_kernel(x_ref, xh_ref, oa_ref, ob_ref, gpre_ref, gpost_ref, wu_ref, wg_ref, wa_ref, wb_ref, wc_ref,
                  wout_ref, wpool_ref, pscale_ref, o_ref):
    t = pl.program_id(1)
    x = x_ref[0]
    h = _rms(x, gpre_ref[...]).astype(BF16)
    hh = _rms(xh_ref[0], gpre_ref[...]).astype(BF16)
    u = _dot(h, wu_ref[...])
    uh = jnp.where(t == 0, 0.0, _dot(hh, wu_ref[...]))
    u_ext = jnp.concatenate([uh, u], axis=0)
    pos = (t * TOK_TILE + lax.broadcasted_iota(jnp.int32, (TOK_TILE, POOL_GROUP_DIM), 0) + 1).astype(F32)
    oc = []
    for g, w in enumerate(POOL_WINDOWS):
        cols = slice(g * POOL_GROUP_DIM, (g + 1) * POOL_GROUP_DIM)
        acc = u_ext[:, cols]
        span = 1
        while span < w:
            acc = acc + pltpu.roll(acc, span, 0)
            span *= 2
        pooled = acc[POOL_HALO:] / jnp.minimum(pos, float(w)) - u[:, cols]
        oc.append(_dot(pooled.astype(BF16), wpool_ref[g]))
    oc = (jnp.concatenate(oc, axis=1) * pscale_ref[...]).astype(BF16)

    merged = None
    for br, (src, w_ref) in enumerate(((oa_ref[0], wa_ref), (ob_ref[0], wb_ref), (oc, wc_ref))):
        gz = _dot(h, wg_ref[:, br * D_MODEL:(br + 1) * D_MODEL])
        term = (1.0 / (1.0 + jnp.exp(-gz))) * _dot(src, w_ref[...])
        merged = term if merged is None else merged + term
    y = _dot(merged.astype(BF16), wout_ref[...])
    o_ref[0] = x + _rms(y, gpost_ref[...])


def _merge(x, oa, ob, gpre, gpost, wu, wg, wa, wb, wc, wout, wpool, pscale):
    b, s, d = x.shape
    const = lambda shape: pl.BlockSpec(shape, lambda bi, ti: (0,) * len(shape))
    halo_blocks = TOK_TILE // POOL_HALO
    return pl.pallas_call(
        _merge_kernel,
        grid=(b, s // TOK_TILE),
        in_specs=[
            pl.BlockSpec((1, TOK_TILE, d), lambda bi, ti: (bi, ti, 0)),
            pl.BlockSpec((1, POOL_HALO, d), lambda bi, ti: (bi, jnp.maximum(ti * halo_blocks - 1, 0), 0)),
            pl.BlockSpec((1, TOK_TILE, D_ATT), lambda bi, ti: (bi, ti, 0)),
            pl.BlockSpec((1, TOK_TILE, D_ATT), lambda bi, ti: (bi, ti, 0)),
            const((1, d)), const((1, d)),
            const((d, POOL_WIDTH)), const((d, 3 * d)),
            const((D_ATT, d)), const((D_ATT, d)), const((POOL_WIDTH, d)),
            const((d, d)), const((len(POOL_WINDOWS), POOL_GROUP_DIM, POOL_GROUP_DIM)), const((1, POOL_WIDTH)),
        ],
        out_specs=pl.BlockSpec((1, TOK_TILE, d), lambda bi, ti: (bi, ti, 0)),
        out_shape=jax.ShapeDtypeStruct((b, s, d), F32),
        compiler_params=pltpu.CompilerParams(
            dimension_semantics=("arbitrary", "arbitrary"), vmem_limit_bytes=VMEM_LIMIT),
        name="merge",
    )(x, x, oa, ob, gpre, gpost, wu, wg, wa, wb, wc, wout, wpool, pscale)


def _mlp_kernel(x_ref, gpre_ref, gpost_ref, w1_ref, w2_ref, o_ref):
    x = x_ref[0]
    h = _rms(x, gpre_ref[...]).astype(BF16)
    y = None
    for c in range(D_FF // D_MODEL):
        cols = slice(c * D_MODEL, (c + 1) * D_MODEL)
        a = jnp.maximum(_dot(h, w1_ref[:, cols]), 0.0)
        part = _dot((a * a).astype(BF16), w2_ref[cols, :])
        y = part if y is None else y + part
    o_ref[0] = x + _rms(y, gpost_ref[...])


def _mlp(x, gpre, gpost, w1, w2):
    b, s, d = x.shape
    const = lambda shape: pl.BlockSpec(shape, lambda bi, ti: (0,) * len(shape))
    return pl.pallas_call(
        _mlp_kernel,
        grid=(b, s // TOK_TILE),
        in_specs=[pl.BlockSpec((1, TOK_TILE, d), lambda bi, ti: (bi, ti, 0)),
                  const((1, d)), const((1, d)), const((d, D_FF)), const((D_FF, d))],
        out_specs=pl.BlockSpec((1, TOK_TILE, d), lambda bi, ti: (bi, ti, 0)),
        out_shape=jax.ShapeDtypeStruct((b, s, d), F32),
        compiler_params=pltpu.CompilerParams(
            dimension_semantics=("arbitrary", "arbitrary"), vmem_limit_bytes=VMEM_LIMIT),
        name="mlp",
    )(x, gpre, gpost, w1, w2)


def _layer(x, w_in, w_br_a, w_br_b, w_br_c, w_out, w_pool, pool_scale,
           g_pre_mix, g_post_mix, g_pre_mlp, g_post_mlp, w_ff1, w_ff2):
    col = lambda k: slice(k * D_ATT, (k + 1) * D_ATT)
    w_qa, w_ka, w_va = w_in[:, col(0)], w_in[:, col(1)], w_in[:, col(2)]
    wk_hi, wk_lo = _split_bf16(w_ka)
    wqt_hi, wqt_lo = _split_bf16(w_qa.T)
    wvt = w_va.T.astype(BF16)
    wtok = w_in[:, 3 * D_ATT:6 * D_ATT].astype(BF16)
    wu = w_in[:, 6 * D_ATT:6 * D_ATT + POOL_WIDTH].astype(BF16)
    wg = w_in[:, 6 * D_ATT + POOL_WIDTH:].astype(BF16)
    row = lambda v: v.reshape(1, -1)

    ka, qkvb, qt, bias, vt = _in_proj(x, row(g_pre_mix), wk_hi, wk_lo, wtok, wqt_hi, wqt_lo, wvt)
    oa = _moba(qt, bias, ka, vt)
    ob = _dilated(qkvb)
    x = _merge(x, oa, ob, row(g_pre_mix), row(g_post_mix), wu, wg,
               w_br_a.astype(BF16), w_br_b.astype(BF16), w_br_c.astype(BF16), w_out.astype(BF16),
               w_pool.astype(BF16), row(pool_scale))
    return _mlp(x, row(g_pre_mlp), row(g_post_mlp), w_ff1.astype(BF16), w_ff2.astype(BF16))


@jax.jit
def kernel(x, w_in, w_br_a, w_br_b, w_br_c, w_out, w_pool, pool_scale, g_pre_mix, g_post_mix, g_pre_mlp, g_post_mlp, w_ff1, w_ff2):
    for l in range(w_in.shape[0]):
        x = _layer(x, w_in[l], w_br_a[l], w_br_b[l], w_br_c[l], w_out[l], w_pool[l], pool_scale[l],
                   g_pre_mix[l], g_post_mix[l], g_pre_mlp[l], g_post_mlp[l], w_ff1[l], w_ff2[l])
    return x
```

```python
import jax
import jax.numpy as jnp
from jax import lax
from jax.experimental import pallas as pl
from jax.experimental.pallas import tpu as pltpu

F32 = jnp.float32
BF16 = jnp.bfloat16

D_MODEL = 1024
HEAD_DIM = 64
N_HEADS = 8
D_ATT = N_HEADS * HEAD_DIM
PAIR = 2 * HEAD_DIM
N_PAIRS = N_HEADS // 2
MOBA_BLOCK = 256
MOBA_TOPK = 3
MOBA_GROUP = 4
MOBA_QTILE = 512
DENOM_ROWS = 16
VT_ROWS = HEAD_DIM + DENOM_ROWS
LOG2_E = 1.4426950408889634
DIL_PATTERNS = ((128, 1), (512, 4), (2048, 16))
DIL_BAND = 128
DIL_TILE = 2048
POOL_WINDOWS = (2, 4, 8, 16)
POOL_WIDTH = 512
POOL_GROUP_DIM = 128
POOL_HALO = 16
D_FF = 4 * D_MODEL
RMS_EPS = 1e-6
QK_SCALE = HEAD_DIM ** -0.5
NEG = -1e30

TOK_TILE = 512
VMEM_LIMIT = 56 * 1024 * 1024

NT_DIMS = (((1,), (1,)), ((), ()))


def _dot(a, b):
    return jnp.dot(a, b, preferred_element_type=F32)


def _dot_nt(a, b):
    return lax.dot_general(a, b, NT_DIMS, preferred_element_type=F32)


def _split_bf16(v):
    hi = v.astype(BF16)
    lo = (v - hi.astype(F32)).astype(BF16)
    return hi, lo


def _rms(x, g):
    var = jnp.mean(x * x, axis=-1, keepdims=True)
    return x * lax.rsqrt(var + RMS_EPS) * g


def _in_proj_kernel(x_ref, g_ref, wk_hi_ref, wk_lo_ref, wtok_ref, wqt_hi_ref, wqt_lo_ref, wvt_ref,
                    ka_ref, qkvb_ref, qt_ref, bias_ref, vt_ref, kbar_ref):
    t = pl.program_id(1)
    blocks_per_tile = TOK_TILE // MOBA_BLOCK

    @pl.when(t == 0)
    def _():
        kbar_ref[...] = jnp.zeros_like(kbar_ref)

    h = _rms(x_ref[0], g_ref[...])
    h_hi, h_lo = _split_bf16(h)

    ka = _dot(h_hi, wk_hi_ref[...]) + _dot(h_hi, wk_lo_ref[...]) + _dot(h_lo, wk_hi_ref[...])
    ka_ref[0] = ka.astype(BF16)
    for blk in range(blocks_per_tile):
        kbar_ref[pl.ds(t * blocks_per_tile + blk, 1), :] = jnp.mean(
            ka[blk * MOBA_BLOCK:(blk + 1) * MOBA_BLOCK], axis=0, keepdims=True)

    qkvb = _dot(h_hi, wtok_ref[...])
    qkvb_ref[0, :, 0:D_ATT] = qkvb[:, 0:D_ATT] * QK_SCALE
    qkvb_ref[0, :, D_ATT:3 * D_ATT] = qkvb[:, D_ATT:3 * D_ATT]

    qt = (_dot_nt(wqt_hi_ref[...], h_hi) + _dot_nt(wqt_lo_ref[...], h_hi)
          + _dot_nt(wqt_hi_ref[...], h_lo)) * (QK_SCALE * LOG2_E)
    qt_ref[0] = qt.astype(BF16)
    vt = _dot_nt(wvt_ref[...], h_hi).astype(BF16)
    ones_rows = jnp.ones((DENOM_ROWS, TOK_TILE), BF16)
    for head in range(N_HEADS):
        vt_ref[0, head * VT_ROWS:head * VT_ROWS + HEAD_DIM, :] = vt[head * HEAD_DIM:(head + 1) * HEAD_DIM]
        vt_ref[0, head * VT_ROWS + HEAD_DIM:(head + 1) * VT_ROWS, :] = ones_rows

    half = D_ATT // 2
    r = lax.broadcasted_iota(jnp.int32, (half, half), 0) // HEAD_DIM
    c = lax.broadcasted_iota(jnp.int32, (half, half), 1) // HEAD_DIM
    same_head = r == c
    n_blocks = kbar_ref.shape[0]
    blk_row = lax.broadcasted_iota(jnp.int32, (n_blocks, TOK_TILE), 0).astype(F32)
    q_blk = (t * blocks_per_tile
             + lax.broadcasted_iota(jnp.int32, (n_blocks, TOK_TILE), 1) // MOBA_BLOCK).astype(F32)
    valid = blk_row < q_blk
    for quad in range(2):
        kq = kbar_ref[:, quad * half:(quad + 1) * half]
        kb = jnp.where(same_head, jnp.concatenate([kq] * 4, axis=0), 0.0)
        kb_hi, kb_lo = _split_bf16(kb)
        q_hi, q_lo = _split_bf16(qt[quad * half:(quad + 1) * half])
        gs = _dot(kb_hi, q_hi) + _dot(kb_hi, q_lo) + _dot(kb_lo, q_hi)
        for hq in range(4):
            g = jnp.where(valid, gs[hq * n_blocks:(hq + 1) * n_blocks], -jnp.inf)
            sel = jnp.zeros(g.shape, jnp.bool_)
            for _ in range(MOBA_TOPK):
                m = jnp.max(g, axis=0, keepdims=True)
                idx = jnp.min(jnp.where(g == m, blk_row, float(n_blocks)), axis=0, keepdims=True)
                pick = blk_row == idx
                sel = jnp.logical_or(sel, pick)
                g = jnp.where(pick, -jnp.inf, g)
            head = quad * 4 + hq
            bias_ref[0, head * n_blocks:(head + 1) * n_blocks, :] = jnp.where(
                jnp.logical_and(sel, valid), 0.0, NEG)


def _in_proj(x, g, wk_hi, wk_lo, wtok, wqt_hi, wqt_lo, wvt):
    b, s, d = x.shape
    n_blocks = s // MOBA_BLOCK
    const = lambda shape: pl.BlockSpec(shape, lambda bi, ti: (0,) * len(shape))
    return pl.pallas_call(
        _in_proj_kernel,
        grid=(b, s // TOK_TILE),
        in_specs=[
            pl.BlockSpec((1, TOK_TILE, d), lambda bi, ti: (bi, ti, 0)),
            const((1, d)),
            const((d, D_ATT)), const((d, D_ATT)), const((d, 3 * D_ATT)),
            const((D_ATT, d)), const((D_ATT, d)), const((D_ATT, d)),
        ],
        out_specs=[
            pl.BlockSpec((1, TOK_TILE, D_ATT), lambda bi, ti: (bi, ti, 0)),
            pl.BlockSpec((1, TOK_TILE, 3 * D_ATT), lambda bi, ti: (bi, ti, 0)),
            pl.BlockSpec((1, D_ATT, TOK_TILE), lambda bi, ti: (bi, 0, ti)),
            pl.BlockSpec((1, N_HEADS * n_blocks, TOK_TILE), lambda bi, ti: (bi, 0, ti)),
            pl.BlockSpec((1, N_HEADS * VT_ROWS, TOK_TILE), lambda bi, ti: (bi, 0, ti)),
        ],
        out_shape=[
            jax.ShapeDtypeStruct((b, s, D_ATT), BF16),
            jax.ShapeDtypeStruct((b, s, 3 * D_ATT), F32),
            jax.ShapeDtypeStruct((b, D_ATT, s), BF16),
            jax.ShapeDtypeStruct((b, N_HEADS * n_blocks, s), F32),
            jax.ShapeDtypeStruct((b, N_HEADS * VT_ROWS, s), BF16),
        ],
        scratch_shapes=[pltpu.VMEM((n_blocks, D_ATT), F32)],
        compiler_params=pltpu.CompilerParams(
            dimension_semantics=("arbitrary", "arbitrary"), vmem_limit_bytes=VMEM_LIMIT),
        name="in_proj",
    )(x, g, wk_hi, wk_lo, wtok, wqt_hi, wqt_lo, wvt)


def _moba_kernel(qt_ref, bias_ref, k_ref, vt_ref, o_ref, s_sc, p_sc):
    n_blocks = bias_ref.shape[1] // 2
    n_groups = n_blocks // MOBA_GROUP
    group_keys = MOBA_GROUP * MOBA_BLOCK
    first_blk = pl.program_id(2) * (MOBA_QTILE // MOBA_BLOCK)
    own_g = first_blk // MOBA_GROUP
    qt = qt_ref[0]
    zeros = jnp.zeros((HEAD_DIM, MOBA_QTILE), BF16)
    qt_heads = (jnp.concatenate([qt[:HEAD_DIM], zeros], axis=0),
                jnp.concatenate([zeros, qt[HEAD_DIM:]], axis=0))
    key_pos = lax.broadcasted_iota(jnp.int32, (MOBA_BLOCK, MOBA_QTILE), 0)
    qry_pos = lax.broadcasted_iota(jnp.int32, (MOBA_BLOCK, MOBA_QTILE), 1)
    future = key_pos > qry_pos % MOBA_BLOCK
    qry_blk = qry_pos // MOBA_BLOCK
    qry_blk_row = qry_blk[0:1]

    def scores(g, slot, own_group=False):
        k_g = k_ref[0, pl.ds(pl.multiple_of(g * group_keys, group_keys), group_keys), :]
        col_max = []
        for hd in range(2):
            s = _dot(k_g, qt_heads[hd])
            for r in range(MOBA_GROUP):
                s_r = s[r * MOBA_BLOCK:(r + 1) * MOBA_BLOCK]
                if own_group:
                    rel = g * MOBA_GROUP + r - first_blk
                    s_r = s_r + jnp.where(jnp.logical_and(qry_blk == rel, future), NEG, 0.0)
                s_sc[slot, hd, r * MOBA_BLOCK:(r + 1) * MOBA_BLOCK, :] = s_r
                col_max.append(jnp.max(s_r, axis=0, keepdims=True))
        return tuple(col_max)

    def softmax(g, slot, m_pair, col_max, own_group=False, penalty=None):
        alphas, m_news = [], []
        for hd in range(2):
            gates = []
            for r in range(MOBA_GROUP):
                gate = bias_ref[0, pl.ds(hd * n_blocks + g * MOBA_GROUP + r, 1), :]
                if own_group:
                    gate = jnp.where(qry_blk_row == g * MOBA_GROUP + r - first_blk, 0.0, gate)
                if penalty is not None:
                    gate = gate + penalty
                gates.append(gate)
            m_new = m_pair[hd]
            for r, gate in enumerate(gates):
                m_new = jnp.maximum(m_new, col_max[hd * MOBA_GROUP + r] + gate)
            alphas.append(jnp.exp2(m_pair[hd] - m_new))
            m_news.append(m_new)
            for r, gate in enumerate(gates):
                rows = slice(r * MOBA_BLOCK, (r + 1) * MOBA_BLOCK)
                p_sc[slot, hd, rows, :] = jnp.exp2(s_sc[slot, hd, rows, :] - (m_new - gate)).astype(BF16)
        return tuple(alphas), tuple(m_news)

    def accum(g, slot, alphas, accs):
        vt_g = vt_ref[0, :, pl.ds(pl.multiple_of(g * group_keys, group_keys), group_keys)]
        return tuple(alphas[hd] * accs[hd] + _dot(vt_g[hd * VT_ROWS:(hd + 1) * VT_ROWS], p_sc[slot, hd])
                     for hd in range(2))

    m0 = (jnp.full((1, MOBA_QTILE), NEG, F32),) * 2
    acc0 = (jnp.zeros((VT_ROWS, MOBA_QTILE), F32),) * 2
    cm_own = scores(own_g, 0, own_group=True)
    alpha0, m1 = softmax(own_g, 0, m0, cm_own, own_group=True)
    cm0 = scores(0, 1)

    def two_steps(d, carry):
        alpha_prev, m, acc, cm_a = carry
        g_a = 2 * d
        g_b = jnp.minimum(g_a + 1, n_groups - 1)
        g_prev = jnp.where(d == 0, own_g, g_a - 1)
        cm_b = scores(g_b, 0)
        alpha_a, m = softmax(g_a, 1, m, cm_a)
        acc = accum(g_prev, 0, alpha_prev, acc)
        cm_next = scores(jnp.minimum(g_a + 2, n_groups - 1), 1)
        padded = g_a + 1 >= own_g
        alpha_b, m = softmax(g_b, 0, m, cm_b, penalty=jnp.where(padded, NEG, 0.0))
        acc = accum(g_a, 1, alpha_a, acc)
        return alpha_b, m, acc, cm_next

    n_double = (own_g + 1) // 2
    alpha_l, _, acc, _ = lax.fori_loop(0, n_double, two_steps, (alpha0, m1, acc0, cm0))
    g_last = jnp.where(n_double == 0, own_g, jnp.minimum(2 * n_double - 1, n_groups - 1))
    acc = accum(g_last, 0, alpha_l, acc)
    ot = jnp.concatenate([a[:HEAD_DIM] * (1.0 / a[HEAD_DIM:HEAD_DIM + 1]) for a in acc], axis=0)
    o_ref[0] = ot.T.astype(BF16)


def _moba(qt, bias, ka, vt):
    b, s, _ = ka.shape
    n_blocks = s // MOBA_BLOCK
    return pl.pallas_call(
        _moba_kernel,
        grid=(b, N_PAIRS, s // MOBA_QTILE),
        in_specs=[
            pl.BlockSpec((1, PAIR, MOBA_QTILE), lambda bi, p, i: (bi, p, i)),
            pl.BlockSpec((1, 2 * n_blocks, MOBA_QTILE), lambda bi, p, i: (bi, p, i)),
            pl.BlockSpec((1, s, PAIR), lambda bi, p, i: (bi, 0, p)),
            pl.BlockSpec((1, 2 * VT_ROWS, s), lambda bi, p, i: (bi, p, 0)),
        ],
        out_specs=pl.BlockSpec((1, MOBA_QTILE, PAIR), lambda bi, p, i: (bi, i, p)),
        out_shape=jax.ShapeDtypeStruct((b, s, D_ATT), BF16),
        scratch_shapes=[pltpu.VMEM((2, 2, MOBA_GROUP * MOBA_BLOCK, MOBA_QTILE), F32),
                        pltpu.VMEM((2, 2, MOBA_GROUP * MOBA_BLOCK, MOBA_QTILE), BF16)],
        compiler_params=pltpu.CompilerParams(
            dimension_semantics=("arbitrary", "arbitrary", "arbitrary"), vmem_limit_bytes=VMEM_LIMIT),
        name="moba",
    )(qt, bias, ka, vt)


def _dilated_kernel(q_ref, kc_ref, kp_ref, vc_ref, vp_ref, o_ref, o_sc, lse_sc):
    first_tile = pl.program_id(2) == 0
    n_tiles = DIL_TILE // DIL_BAND
    head_a = lax.broadcasted_iota(jnp.int32, (DIL_TILE, PAIR), 1) < HEAD_DIM
    qi = lax.broadcasted_iota(jnp.int32, (DIL_BAND, 2 * DIL_BAND), 0)
    kj = lax.broadcasted_iota(jnp.int32, (DIL_BAND, 2 * DIL_BAND), 1)
    band = jnp.where(jnp.logical_and(kj >= qi, kj <= qi + DIL_BAND), 0.0, NEG)
    band_start = jnp.where(first_tile, jnp.where(kj >= DIL_BAND, band, NEG), band)

    for pat, (window, dil) in enumerate(DIL_PATTERNS):
        rows = DIL_TILE // dil
        n_q = rows // DIL_BAND
        strided = lambda ref, start, size: ref[0, pl.ds(start, size, stride=dil), :]
        q_parts, k_tiles, v_tiles, masks = [], [], [], []
        for r in range(dil):
            q_parts.append(strided(q_ref, r, rows))
            prev_start = DIL_TILE - DIL_BAND * dil + r
            k_ext = jnp.concatenate([strided(kp_ref, prev_start, DIL_BAND), strided(kc_ref, r, rows)],
                                    axis=0).astype(BF16)
            v_ext = jnp.concatenate([strided(vp_ref, prev_start, DIL_BAND), strided(vc_ref, r, rows)],
                                    axis=0).astype(BF16)
            for qb in range(n_q):
                k_tiles.append(k_ext[qb * DIL_BAND:(qb + 2) * DIL_BAND])
                v_tiles.append(v_ext[qb * DIL_BAND:(qb + 2) * DIL_BAND])
                masks.append(band_start if qb == 0 else band)
        q_all = jnp.concatenate(q_parts, axis=0)
        mask_all = jnp.concatenate(masks, axis=0)
        outs, lses = [], []
        for hd in range(2):
            qh = jnp.where(head_a if hd == 0 else jnp.logical_not(head_a), q_all, 0.0).astype(BF16)
            s = jnp.concatenate(
                [_dot_nt(qh[n * DIL_BAND:(n + 1) * DIL_BAND], k_tiles[n]) for n in range(n_tiles)],
                axis=0) + mask_all
            m = jnp.max(s, axis=-1, keepdims=True)
            p = jnp.exp(s - m)
            l = jnp.sum(p, axis=-1, keepdims=True)
            pb = p.astype(BF16)
            o = jnp.concatenate(
                [_dot(pb[n * DIL_BAND:(n + 1) * DIL_BAND], v_tiles[n]) for n in range(n_tiles)], axis=0)
            outs.append(o / l)
            lses.append(m + jnp.log(l))
        o_pair = jnp.where(head_a, outs[0], outs[1])
        lse_pair = jnp.where(head_a, lses[0], lses[1])
        for r in range(dil):
            o_sc[pat, pl.ds(r, rows, stride=dil), :] = o_pair[r * rows:(r + 1) * rows]
            lse_sc[pat, pl.ds(r, rows, stride=dil), :] = lse_pair[r * rows:(r + 1) * rows]

    lse = lse_sc[...]
    w = jnp.exp(lse - jnp.max(lse, axis=0, keepdims=True))
    o_ref[0] = (jnp.sum(w * o_sc[...], axis=0) / jnp.sum(w, axis=0)).astype(BF16)


def _dilated(qkvb):
    b, s, _ = qkvb.shape
    tile_spec = lambda col0, prev: pl.BlockSpec(
        (1, DIL_TILE, PAIR),
        (lambda bi, p, t: (bi, jnp.maximum(t - 1, 0), col0 + p)) if prev else (lambda bi, p, t: (bi, t, col0 + p)))
    return pl.pallas_call(
        _dilated_kernel,
        grid=(b, N_PAIRS, s // DIL_TILE),
        in_specs=[tile_spec(0, False),
                  tile_spec(N_PAIRS, False), tile_spec(N_PAIRS, True),
                  tile_spec(2 * N_PAIRS, False), tile_spec(2 * N_PAIRS, True)],
        out_specs=pl.BlockSpec((1, DIL_TILE, PAIR), lambda bi, p, t: (bi, t, p)),
        out_shape=jax.ShapeDtypeStruct((b, s, D_ATT), BF16),
        scratch_shapes=[pltpu.VMEM((len(DIL_PATTERNS), DIL_TILE, PAIR), F32),
                        pltpu.VMEM((len(DIL_PATTERNS), DIL_TILE, PAIR), F32)],
        compiler_params=pltpu.CompilerParams(
            dimension_semantics=("arbitrary", "arbitrary", "arbitrary"), vmem_limit_bytes=VMEM_LIMIT),
        name="dilated",
    )(qkvb, qkvb, qkvb, qkvb, qkvb)


def _merge_kernel(x_ref, xh_ref, oa_ref, ob_ref, gpre_ref, gpost_ref, wu_ref, wg_ref, wa_ref, wb_ref, wc_ref,
                  wout_ref, wpool_ref, pscale_ref, o_ref):
    t = pl.program_id(1)
    x = x_ref[0]
    h = _rms(x, gpre_ref[...]).astype(BF16)
    hh = _rms(xh_ref[0], gpre_ref[...]).astype(BF16)
    u = _dot(h, wu_ref[...])
    uh = jnp.where(t == 0, 0.0, _dot(hh, wu_ref[...]))
    u_ext = jnp.concatenate([uh, u], axis=0)
    pos = (t * TOK_TILE + lax.broadcasted_iota(jnp.int32, (TOK_TILE, POOL_GROUP_DIM), 0) + 1).astype(F32)
    oc = []
    for g, w in enumerate(POOL_WINDOWS):
        cols = slice(g * POOL_GROUP_DIM, (g + 1) * POOL_GROUP_DIM)
        acc = u_ext[:, cols]
        span = 1
        while span < w:
            acc = acc + pltpu.roll(acc, span, 0)
            span *= 2
        pooled = acc[POOL_HALO:] / jnp.minimum(pos, float(w)) - u[:, cols]
        oc.append(_dot(pooled.astype(BF16), wpool_ref[g]))
    oc = (jnp.concatenate(oc, axis=1) * pscale_ref[...]).astype(BF16)

    merged = None
    for br, (src, w_ref) in enumerate(((oa_ref[0], wa_ref), (ob_ref[0], wb_ref), (oc, wc_ref))):
        gz = _dot(h, wg_ref[:, br * D_MODEL:(br + 1) * D_MODEL])
        term = (1.0 / (1.0 + jnp.exp(-gz))) * _dot(src, w_ref[...])
        merged = term if merged is None else merged + term
    y = _dot(merged.astype(BF16), wout_ref[...])
    o_ref[0] = x + _rms(y, gpost_ref[...])


def _merge(x, oa, ob, gpre, gpost, wu, wg, wa, wb, wc, wout, wpool, pscale):
    b, s, d = x.shape
    const = lambda shape: pl.BlockSpec(shape, lambda bi, ti: (0,) * len(shape))
    halo_blocks = TOK_TILE // POOL_HALO
    return pl.pallas_call(
        _merge_kernel,
        grid=(b, s // TOK_TILE),
        in_specs=[
            pl.BlockSpec((1, TOK_TILE, d), lambda bi, ti: (bi, ti, 0)),
            pl.BlockSpec((1, POOL_HALO, d), lambda bi, ti: (bi, jnp.maximum(ti * halo_blocks - 1, 0), 0)),
            pl.BlockSpec((1, TOK_TILE, D_ATT), lambda bi, ti: (bi, ti, 0)),
            pl.BlockSpec((1, TOK_TILE, D_ATT), lambda bi, ti: (bi, ti, 0)),
            const((1, d)), const((1, d)),
            const((d, POOL_WIDTH)), const((d, 3 * d)),
            const((D_ATT, d)), const((D_ATT, d)), const((POOL_WIDTH, d)),
            const((d, d)), const((len(POOL_WINDOWS), POOL_GROUP_DIM, POOL_GROUP_DIM)), const((1, POOL_WIDTH)),
        ],
        out_specs=pl.BlockSpec((1, TOK_TILE, d), lambda bi, ti: (bi, ti, 0)),
        out_shape=jax.ShapeDtypeStruct((b, s, d), F32),
        compiler_params=pltpu.CompilerParams(
            dimension_semantics=("arbitrary", "arbitrary"), vmem_limit_bytes=VMEM_LIMIT),
        name="merge",
    )(x, x, oa, ob, gpre, gpost, wu, wg, wa, wb, wc, wout, wpool, pscale)


def _mlp_kernel(x_ref, gpre_ref, gpost_ref, w1_ref, w2_ref, o_ref):
    x = x_ref[0]
    h = _rms(x, gpre_ref[...]).astype(BF16)
    y = None
    for c in range(D_FF // D_MODEL):
        cols = slice(c * D_MODEL, (c + 1) * D_MODEL)
        a = jnp.maximum(_dot(h, w1_ref[:, cols]), 0.0)
        part = _dot((a * a).astype(BF16), w2_ref[cols, :])
        y = part if y is None else y + part
    o_ref[0] = x + _rms(y, gpost_ref[...])


def _mlp(x, gpre, gpost, w1, w2):
    b, s, d = x.shape
    const = lambda shape: pl.BlockSpec(shape, lambda bi, ti: (0,) * len(shape))
    return pl.pallas_call(
        _mlp_kernel,
        grid=(b, s // TOK_TILE),
        in_specs=[pl.BlockSpec((1, TOK_TILE, d), lambda bi, ti: (bi, ti, 0)),
                  const((1, d)), const((1, d)), const((d, D_FF)), const((D_FF, d))],
        out_specs=pl.BlockSpec((1, TOK_TILE, d), lambda bi, ti: (bi, ti, 0)),
        out_shape=jax.ShapeDtypeStruct((b, s, d), F32),
        compiler_params=pltpu.CompilerParams(
            dimension_semantics=("arbitrary", "arbitrary"), vmem_limit_bytes=VMEM_LIMIT),
        name="mlp",
    )(x, gpre, gpost, w1, w2)


def _layer(x, w_in, w_br_a, w_br_b, w_br_c, w_out, w_pool, pool_scale,
           g_pre_mix, g_post_mix, g_pre_mlp, g_post_mlp, w_ff1, w_ff2):
    col = lambda k: slice(k * D_ATT, (k + 1) * D_ATT)
    w_qa, w_ka, w_va = w_in[:, col(0)], w_in[:, col(1)], w_in[:, col(2)]
    wk_hi, wk_lo = _split_bf16(w_ka)
    wqt_hi, wqt_lo = _split_bf16(w_qa.T)
    wvt = w_va.T.astype(BF16)
    wtok = w_in[:, 3 * D_ATT:6 * D_ATT].astype(BF16)
    wu = w_in[:, 6 * D_ATT:6 * D_ATT + POOL_WIDTH].astype(BF16)
    wg = w_in[:, 6 * D_ATT + POOL_WIDTH:].astype(BF16)
    row = lambda v: v.reshape(1, -1)

    ka, qkvb, qt, bias, vt = _in_proj(x, row(g_pre_mix), wk_hi, wk_lo, wtok, wqt_hi, wqt_lo, wvt)
    oa = _moba(qt, bias, ka, vt)
    ob = _dilated(qkvb)
    x = _merge(x, oa, ob, row(g_pre_mix), row(g_post_mix), wu, wg,
               w_br_a.astype(BF16), w_br_b.astype(BF16), w_br_c.astype(BF16), w_out.astype(BF16),
               w_pool.astype(BF16), row(pool_scale))
    return _mlp(x, row(g_pre_mlp), row(g_post_mlp), w_ff1.astype(BF16), w_ff2.astype(BF16))


@jax.jit
def kernel(x, w_in, w_br_a, w_br_b, w_br_c, w_out, w_pool, pool_scale, g_pre_mix, g_post_mix, g_pre_mlp, g_post_mlp, w_ff1, w_ff2):
    for l in range(w_in.shape[0]):
        x = _layer(x, w_in[l], w_br_a[l], w_br_b[l], w_br_c[l], w_out[l], w_pool[l], pool_scale[l],
                   g_pre_mix[l], g_post_mix[l], g_pre_mlp[l], g_post_mlp[l], w_ff1[l], w_ff2[l])
    return x
```

```python
import jax
import jax.numpy as jnp
from jax import lax
from jax.experimental import pallas as pl
from jax.experimental.pallas import tpu as pltpu

F32 = jnp.float32
BF16 = jnp.bfloat16

D_MODEL = 1024
HEAD_DIM = 64
N_HEADS = 8
D_ATT = N_HEADS * HEAD_DIM
PAIR = 2 * HEAD_DIM
N_PAIRS = N_HEADS // 2
MOBA_BLOCK = 256
MOBA_TOPK = 3
MOBA_GROUP = 4
MOBA_QTILE = 512
DENOM_ROWS = 16
VT_ROWS = HEAD_DIM + DENOM_ROWS
LOG2_E = 1.4426950408889634
DIL_PATTERNS = ((128, 1), (512, 4), (2048, 16))
DIL_BAND = 128
DIL_TILE = 2048
POOL_WINDOWS = (2, 4, 8, 16)
POOL_WIDTH = 512
POOL_GROUP_DIM = 128
POOL_HALO = 16
D_FF = 4 * D_MODEL
RMS_EPS = 1e-6
QK_SCALE = HEAD_DIM ** -0.5
NEG = -1e30

TOK_TILE = 512
VMEM_LIMIT = 56 * 1024 * 1024

NT_DIMS = (((1,), (1,)), ((), ()))


def _dot(a, b):
    return jnp.dot(a, b, preferred_element_type=F32)


def _dot_nt(a, b):
    return lax.dot_general(a, b, NT_DIMS, preferred_element_type=F32)


def _split_bf16(v):
    hi = v.astype(BF16)
    lo = (v - hi.astype(F32)).astype(BF16)
    return hi, lo


def _rms(x, g):
    var = jnp.mean(x * x, axis=-1, keepdims=True)
    return x * lax.rsqrt(var + RMS_EPS) * g


def _in_proj_kernel(x_ref, g_ref, wk_hi_ref, wk_lo_ref, wtok_ref, wqt_hi_ref, wqt_lo_ref, wvt_ref,
                    ka_ref, qkvb_ref, qt_ref, bias_ref, vt_ref, kbar_ref):
    t = pl.program_id(1)
    blocks_per_tile = TOK_TILE // MOBA_BLOCK

    @pl.when(t == 0)
    def _():
        kbar_ref[...] = jnp.zeros_like(kbar_ref)

    h = _rms(x_ref[0], g_ref[...])
    h_hi, h_lo = _split_bf16(h)

    ka = _dot(h_hi, wk_hi_ref[...]) + _dot(h_hi, wk_lo_ref[...]) + _dot(h_lo, wk_hi_ref[...])
    ka_ref[0] = ka.astype(BF16)
    for blk in range(blocks_per_tile):
        kbar_ref[pl.ds(t * blocks_per_tile + blk, 1), :] = jnp.mean(
            ka[blk * MOBA_BLOCK:(blk + 1) * MOBA_BLOCK], axis=0, keepdims=True)

    qkvb = _dot(h_hi, wtok_ref[...])
    qkvb_ref[0, :, 0:D_ATT] = qkvb[:, 0:D_ATT] * QK_SCALE
    qkvb_ref[0, :, D_ATT:3 * D_ATT] = qkvb[:, D_ATT:3 * D_ATT]

    qt = (_dot_nt(wqt_hi_ref[...], h_hi) + _dot_nt(wqt_lo_ref[...], h_hi)
          + _dot_nt(wqt_hi_ref[...], h_lo)) * (QK_SCALE * LOG2_E)
    qt_ref[0] = qt.astype(BF16)
    vt = _dot_nt(wvt_ref[...], h_hi).astype(BF16)
    ones_rows = jnp.ones((DENOM_ROWS, TOK_TILE), BF16)
    for head in range(N_HEADS):
        vt_ref[0, head * VT_ROWS:head * VT_ROWS + HEAD_DIM, :] = vt[head * HEAD_DIM:(head + 1) * HEAD_DIM]
        vt_ref[0, head * VT_ROWS + HEAD_DIM:(head + 1) * VT_ROWS, :] = ones_rows

    half = D_ATT // 2
    r = lax.broadcasted_iota(jnp.int32, (half, half), 0) // HEAD_DIM
    c = lax.broadcasted_iota(jnp.int32, (half, half), 1) // HEAD_DIM
    same_head = r == c
    n_blocks = kbar_ref.shape[0]
    blk_row = lax.broadcasted_iota(jnp.int32, (n_blocks, TOK_TILE), 0).astype(F32)
    q_blk = (t * blocks_per_tile
             + lax.broadcasted_iota(jnp.int32, (n_blocks, TOK_TILE), 1) // MOBA_BLOCK).astype(F32)
    valid = blk_row < q_blk
    for quad in range(2):
        kq = kbar_ref[:, quad * half:(quad + 1) * half]
        kb = jnp.where(same_head, jnp.concatenate([kq] * 4, axis=0), 0.0)
        kb_hi, kb_lo = _split_bf16(kb)
        q_hi, q_lo = _split_bf16(qt[quad * half:(quad + 1) * half])
        gs = _dot(kb_hi, q_hi) + _dot(kb_hi, q_lo) + _dot(kb_lo, q_hi)
        for hq in range(4):
            g = jnp.where(valid, gs[hq * n_blocks:(hq + 1) * n_blocks], -jnp.inf)
            sel = jnp.zeros(g.shape, jnp.bool_)
            for _ in range(MOBA_TOPK):
                m = jnp.max(g, axis=0, keepdims=True)
                idx = jnp.min(jnp.where(g == m, blk_row, float(n_blocks)), axis=0, keepdims=True)
                pick = blk_row == idx
                sel = jnp.logical_or(sel, pick)
                g = jnp.where(pick, -jnp.inf, g)
            head = quad * 4 + hq
            bias_ref[0, head * n_blocks:(head + 1) * n_blocks, :] = jnp.where(
                jnp.logical_and(sel, valid), 0.0, NEG)


def _in_proj(x, g, wk_hi, wk_lo, wtok, wqt_hi, wqt_lo, wvt):
    b, s, d = x.shape
    n_blocks = s // MOBA_BLOCK
    const = lambda shape: pl.BlockSpec(shape, lambda bi, ti: (0,) * len(shape))
    return pl.pallas_call(
        _in_proj_kernel,
        grid=(b, s // TOK_TILE),
        in_specs=[
            pl.BlockSpec((1, TOK_TILE, d), lambda bi, ti: (bi, ti, 0)),
            const((1, d)),
            const((d, D_ATT)), const((d, D_ATT)), const((d, 3 * D_ATT)),
            const((D_ATT, d)), const((D_ATT, d)), const((D_ATT, d)),
        ],
        out_specs=[
            pl.BlockSpec((1, TOK_TILE, D_ATT), lambda bi, ti: (bi, ti, 0)),
            pl.BlockSpec((1, TOK_TILE, 3 * D_ATT), lambda bi, ti: (bi, ti, 0)),
            pl.BlockSpec((1, D_ATT, TOK_TILE), lambda bi, ti: (bi, 0, ti)),
            pl.BlockSpec((1, N_HEADS * n_blocks, TOK_TILE), lambda bi, ti: (bi, 0, ti)),
            pl.BlockSpec((1, N_HEADS * VT_ROWS, TOK_TILE), lambda bi, ti: (bi, 0, ti)),
        ],
        out_shape=[
            jax.ShapeDtypeStruct((b, s, D_ATT), BF16),
            jax.ShapeDtypeStruct((b, s, 3 * D_ATT), F32),
            jax.ShapeDtypeStruct((b, D_ATT, s), BF16),
            jax.ShapeDtypeStruct((b, N_HEADS * n_blocks, s), F32),
            jax.ShapeDtypeStruct((b, N_HEADS * VT_ROWS, s), BF16),
        ],
        scratch_shapes=[pltpu.VMEM((n_blocks, D_ATT), F32)],
        compiler_params=pltpu.CompilerParams(
            dimension_semantics=("arbitrary", "arbitrary"), vmem_limit_bytes=VMEM_LIMIT),
        name="in_proj",
    )(x, g, wk_hi, wk_lo, wtok, wqt_hi, wqt_lo, wvt)


def _moba_kernel(qt_ref, bias_ref, k_ref, vt_ref, o_ref, s_sc, p_sc):
    n_blocks = bias_ref.shape[1] // 2
    q_blocks = MOBA_QTILE // MOBA_BLOCK
    trip_blocks = 2 * MOBA_GROUP
    first_blk = pl.program_id(2) * q_blocks
    n_trips = first_blk // trip_blocks
    head_start = n_trips * trip_blocks
    qt = qt_ref[0]
    zeros = jnp.zeros((HEAD_DIM, MOBA_QTILE), BF16)
    qt_heads = (jnp.concatenate([qt[:HEAD_DIM], zeros], axis=0),
                jnp.concatenate([zeros, qt[HEAD_DIM:]], axis=0))
    key_pos = lax.broadcasted_iota(jnp.int32, (MOBA_BLOCK, MOBA_QTILE), 0)
    qry_pos = lax.broadcasted_iota(jnp.int32, (MOBA_BLOCK, MOBA_QTILE), 1)
    future = key_pos > qry_pos % MOBA_BLOCK
    qry_blk = qry_pos // MOBA_BLOCK
    qry_blk_row = qry_blk[0:1]

    def owner(r, nb, own_tail):
        return r - (nb - q_blocks) if own_tail and r >= nb - q_blocks else None

    def key_rows(start_blk, nb):
        return pl.ds(pl.multiple_of(start_blk * MOBA_BLOCK, q_blocks * MOBA_BLOCK), nb * MOBA_BLOCK)

    def scores(start_blk, nb, slot, own_tail=False):
        k_c = k_ref[0, key_rows(start_blk, nb), :]
        col_max = []
        for hd in range(2):
            s = _dot(k_c, qt_heads[hd])
            for r in range(nb):
                s_r = s[r * MOBA_BLOCK:(r + 1) * MOBA_BLOCK]
                if owner(r, nb, own_tail) is not None:
                    s_r = s_r + jnp.where(jnp.logical_and(qry_blk == owner(r, nb, own_tail), future), NEG, 0.0)
                s_sc[slot, hd, r * MOBA_BLOCK:(r + 1) * MOBA_BLOCK, :] = s_r
                col_max.append(jnp.max(s_r, axis=0, keepdims=True))
        return tuple(col_max)

    def softmax(start_blk, nb, slot, m_pair, col_max, own_tail=False):
        alphas, m_news = [], []
        for hd in range(2):
            gates = []
            for r in range(nb):
                gate = bias_ref[0, pl.ds(hd * n_blocks + start_blk + r, 1), :]
                if owner(r, nb, own_tail) is not None:
                    gate = jnp.where(qry_blk_row == owner(r, nb, own_tail), 0.0, gate)
                gates.append(gate)
            m_new = m_pair[hd]
            for r, gate in enumerate(gates):
                m_new = jnp.maximum(m_new, col_max[hd * nb + r] + gate)
            alphas.append(jnp.exp2(m_pair[hd] - m_new))
            m_news.append(m_new)
            for r, gate in enumerate(gates):
                rows = slice(r * MOBA_BLOCK, (r + 1) * MOBA_BLOCK)
                p_sc[slot, hd, rows, :] = jnp.exp2(s_sc[slot, hd, rows, :] - (m_new - gate)).astype(BF16)
        return tuple(alphas), tuple(m_news)

    def accum(start_blk, slot, alphas, accs):
        vt_c = vt_ref[0, :, key_rows(start_blk, MOBA_GROUP)]
        return tuple(alphas[hd] * accs[hd] + _dot(vt_c[hd * VT_ROWS:(hd + 1) * VT_ROWS], p_sc[slot, hd])
                     for hd in range(2))

    m0 = (jnp.full((1, MOBA_QTILE), NEG, F32),) * 2
    acc0 = (jnp.zeros((VT_ROWS, MOBA_QTILE), F32),) * 2

    def head_part(n_rem):
        def run():
            total = n_rem + q_blocks
            sizes = [total] if total <= MOBA_GROUP else [total - MOBA_GROUP, MOBA_GROUP]
            m, acc, pending, start = m0, acc0, None, head_start
            for c, nb in enumerate(sizes):
                slot = len(sizes) - 1 - c
                last = c == len(sizes) - 1
                cm = scores(start, nb, slot, own_tail=last)
                alpha, m = softmax(start, nb, slot, m, cm, own_tail=last)
                if pending is not None:
                    acc = accum(pending[0], 1, pending[1], acc)
                if nb < MOBA_GROUP:
                    for hd in range(2):
                        p_sc[slot, hd, nb * MOBA_BLOCK:, :] = jnp.zeros(
                            ((MOBA_GROUP - nb) * MOBA_BLOCK, MOBA_QTILE), BF16)
                pending = (start, alpha)
                start = start + nb
            cm_first = scores(0, MOBA_GROUP, 1)
            return pending[1], m, acc, cm_first
        return run

    variants = [head_part(n_rem) for n_rem in range(0, trip_blocks, q_blocks)]
    n_rem = first_blk - head_start
    alpha_h, m_h, acc_h, cm_first = lax.switch(n_rem // q_blocks, variants)
    pending_start = jnp.where(n_rem + q_blocks <= MOBA_GROUP, head_start, first_blk + q_blocks - MOBA_GROUP)

    def trip(t, carry):
        alpha_prev, m, acc, cm_a = carry
        blk_a = t * trip_blocks
        blk_b = blk_a + MOBA_GROUP
        cm_b = scores(blk_b, MOBA_GROUP, 0)
        alpha_a, m = softmax(blk_a, MOBA_GROUP, 1, m, cm_a)
        acc = accum(jnp.where(t == 0, pending_start, blk_a - MOBA_GROUP), 0, alpha_prev, acc)
        cm_next = scores(jnp.minimum(blk_a + trip_blocks, n_blocks - MOBA_GROUP), MOBA_GROUP, 1)
        alpha_b, m = softmax(blk_b, MOBA_GROUP, 0, m, cm_b)
        acc = accum(blk_a, 1, alpha_a, acc)
        return alpha_b, m, acc, cm_next

    alpha_l, _, acc, _ = lax.fori_loop(0, n_trips, trip, (alpha_h, m_h, acc_h, cm_first))
    acc = accum(jnp.where(n_trips == 0, pending_start, head_start - MOBA_GROUP), 0, alpha_l, acc)
    ot = jnp.concatenate([a[:HEAD_DIM] * (1.0 / a[HEAD_DIM:HEAD_DIM + 1]) for a in acc], axis=0)
    o_ref[0] = ot.T.astype(BF16)


def _moba(qt, bias, ka, vt):
    b, s, _ = ka.shape
    n_blocks = s // MOBA_BLOCK
    return pl.pallas_call(
        _moba_kernel,
        grid=(b, N_PAIRS, s // MOBA_QTILE),
        in_specs=[
            pl.BlockSpec((1, PAIR, MOBA_QTILE), lambda bi, p, i: (bi, p, i)),
            pl.BlockSpec((1, 2 * n_blocks, MOBA_QTILE), lambda bi, p, i: (bi, p, i)),
            pl.BlockSpec((1, s, PAIR), lambda bi, p, i: (bi, 0, p)),
            pl.BlockSpec((1, 2 * VT_ROWS, s), lambda bi, p, i: (bi, p, 0)),
        ],
        out_specs=pl.BlockSpec((1, MOBA_QTILE, PAIR), lambda bi, p, i: (bi, i, p)),
        out_shape=jax.ShapeDtypeStruct((b, s, D_ATT), BF16),
        scratch_shapes=[pltpu.VMEM((2, 2, MOBA_GROUP * MOBA_BLOCK, MOBA_QTILE), F32),
                        pltpu.VMEM((2, 2, MOBA_GROUP * MOBA_BLOCK, MOBA_QTILE), BF16)],
        compiler_params=pltpu.CompilerParams(
            dimension_semantics=("arbitrary", "arbitrary", "arbitrary"), vmem_limit_bytes=VMEM_LIMIT),
        name="moba",
    )(qt, bias, ka, vt)


def _dilated_kernel(q_ref, kc_ref, kp_ref, vc_ref, vp_ref, o_ref, o_sc, lse_sc):
    first_tile = pl.program_id(2) == 0
    n_tiles = DIL_TILE // DIL_BAND
    head_a = lax.broadcasted_iota(jnp.int32, (DIL_TILE, PAIR), 1) < HEAD_DIM
    qi = lax.broadcasted_iota(jnp.int32, (DIL_BAND, 2 * DIL_BAND), 0)
    kj = lax.broadcasted_iota(jnp.int32, (DIL_BAND, 2 * DIL_BAND), 1)
    band = jnp.where(jnp.logical_and(kj >= qi, kj <= qi + DIL_BAND), 0.0, NEG)
    band_start = jnp.where(first_tile, jnp.where(kj >= DIL_BAND, band, NEG), band)

    for pat, (window, dil) in enumerate(DIL_PATTERNS):
        rows = DIL_TILE // dil
        n_q = rows // DIL_BAND
        strided = lambda ref, start, size: ref[0, pl.ds(start, size, stride=dil), :]
        q_parts, k_tiles, v_tiles, masks = [], [], [], []
        for r in range(dil):
            q_parts.append(strided(q_ref, r, rows))
            prev_start = DIL_TILE - DIL_BAND * dil + r
            k_ext = jnp.concatenate([strided(kp_ref, prev_start, DIL_BAND), strided(kc_ref, r, rows)],
                                    axis=0).astype(BF16)
            v_ext = jnp.concatenate([strided(vp_ref, prev_start, DIL_BAND), strided(vc_ref, r, rows)],
                                    axis=0).astype(BF16)
            for qb in range(n_q):
                k_tiles.append(k_ext[qb * DIL_BAND:(qb + 2) * DIL_BAND])
                v_tiles.append(v_ext[qb * DIL_BAND:(qb + 2) * DIL_BAND])
                masks.append(band_start if qb == 0 else band)
        q_all = jnp.concatenate(q_parts, axis=0)
        mask_all = jnp.concatenate(masks, axis=0)
        outs, lses = [], []
        for hd in range(2):
            qh = jnp.where(head_a if hd == 0 else jnp.logical_not(head_a), q_all, 0.0).astype(BF16)
            s = jnp.concatenate(
                [_dot_nt(qh[n * DIL_BAND:(n + 1) * DIL_BAND], k_tiles[n]) for n in range(n_tiles)],
                axis=0) + mask_all
            m = jnp.max(s, axis=-1, keepdims=True)
            p = jnp.exp(s - m)
            l = jnp.sum(p, axis=-1, keepdims=True)
            pb = p.astype(BF16)
            o = jnp.concatenate(
                [_dot(pb[n * DIL_BAND:(n + 1) * DIL_BAND], v_tiles[n]) for n in range(n_tiles)], axis=0)
            outs.append(o / l)
            lses.append(m + jnp.log(l))
        o_pair = jnp.where(head_a, outs[0], outs[1])
        lse_pair = jnp.where(head_a, lses[0], lses[1])
        for r in range(dil):
            o_sc[pat, pl.ds(r, rows, stride=dil), :] = o_pair[r * rows:(r + 1) * rows]
            lse_sc[pat, pl.ds(r, rows, stride=dil), :] = lse_pair[r * rows:(r + 1) * rows]

    lse = lse_sc[...]
    w = jnp.exp(lse - jnp.max(lse, axis=0, keepdims=True))
    o_ref[0] = (jnp.sum(w * o_sc[...], axis=0) / jnp.sum(w, axis=0)).astype(BF16)


def _dilated(qkvb):
    b, s, _ = qkvb.shape
    tile_spec = lambda col0, prev: pl.BlockSpec(
        (1, DIL_TILE, PAIR),
        (lambda bi, p, t: (bi, jnp.maximum(t - 1, 0), col0 + p)) if prev else (lambda bi, p, t: (bi, t, col0 + p)))
    return pl.pallas_call(
        _dilated_kernel,
        grid=(b, N_PAIRS, s // DIL_TILE),
        in_specs=[tile_spec(0, False),
                  tile_spec(N_PAIRS, False), tile_spec(N_PAIRS, True),
                  tile_spec(2 * N_PAIRS, False), tile_spec(2 * N_PAIRS, True)],
        out_specs=pl.BlockSpec((1, DIL_TILE, PAIR), lambda bi, p, t: (bi, t, p)),
        out_shape=jax.ShapeDtypeStruct((b, s, D_ATT), BF16),
        scratch_shapes=[pltpu.VMEM((len(DIL_PATTERNS), DIL_TILE, PAIR), F32),
                        pltpu.VMEM((len(DIL_PATTERNS), DIL_TILE, PAIR), F32)],
        compiler_params=pltpu.CompilerParams(
            dimension_semantics=("arbitrary", "arbitrary", "arbitrary"), vmem_limit_bytes=VMEM_LIMIT),
        name="dilated",
    )(qkvb, qkvb, qkvb, qkvb, qkvb)


def _merge_kernel(x_ref, xh_ref, oa_ref, ob_ref, gpre_ref, gpost_ref, wu_ref, wg_ref, wa_ref, wb_ref, wc_ref,
                  wout_ref, wpool_ref, pscale_ref, o_ref):
    t = pl.program_id(1)
    x = x_ref[0]
    h = _rms(x, gpre_ref[...]).astype(BF16)
    hh = _rms(xh_ref[0], gpre_ref[...]).astype(BF16)
    u = _dot(h, wu_ref[...])
    uh = jnp.where(t == 0, 0.0, _dot(hh, wu_ref[...]))
    u_ext = jnp.concatenate([uh, u], axis=0)
    pos = (t * TOK_TILE + lax.broadcasted_iota(jnp.int32, (TOK_TILE, POOL_GROUP_DIM), 0) + 1).astype(F32)
    oc = []
    for g, w in enumerate(POOL_WINDOWS):
        cols = slice(g * POOL_GROUP_DIM, (g + 1) * POOL_GROUP_DIM)
        acc = u_ext[:, cols]
        span = 1
        while span < w:
            acc = acc + pltpu.roll(acc, span, 0)
            span *= 2
        pooled = acc[POOL_HALO:] / jnp.minimum(pos, float(w)) - u[:, cols]
        oc.append(_dot(pooled.astype(BF16), wpool_ref[g]))
    oc = (jnp.concatenate(oc, axis=1) * pscale_ref[...]).astype(BF16)

    merged = None
    for br, (src, w_ref) in enumerate(((oa_ref[0], wa_ref), (ob_ref[0], wb_ref), (oc, wc_ref))):
        gz = _dot(h, wg_ref[:, br * D_MODEL:(br + 1) * D_MODEL])
        term = (1.0 / (1.0 + jnp.exp(-gz))) * _dot(src, w_ref[...])
        merged = term if merged is None else merged + term
    y = _dot(merged.astype(BF16), wout_ref[...])
    o_ref[0] = x + _rms(y, gpost_ref[...])


def _merge(x, oa, ob, gpre, gpost, wu, wg, wa, wb, wc, wout, wpool, pscale):
    b, s, d = x.shape
    const = lambda shape: pl.BlockSpec(shape, lambda bi, ti: (0,) * len(shape))
    halo_blocks = TOK_TILE // POOL_HALO
    return pl.pallas_call(
        _merge_kernel,
        grid=(b, s // TOK_TILE),
        in_specs=[
            pl.BlockSpec((1, TOK_TILE, d), lambda bi, ti: (bi, ti, 0)),
            pl.BlockSpec((1, POOL_HALO, d), lambda bi, ti: (bi, jnp.maximum(ti * halo_blocks - 1, 0), 0)),
            pl.BlockSpec((1, TOK_TILE, D_ATT), lambda bi, ti: (bi, ti, 0)),
            pl.BlockSpec((1, TOK_TILE, D_ATT), lambda bi, ti: (bi, ti, 0)),
            const((1, d)), const((1, d)),
            const((d, POOL_WIDTH)), const((d, 3 * d)),
            const((D_ATT, d)), const((D_ATT, d)), const((POOL_WIDTH, d)),
            const((d, d)), const((len(POOL_WINDOWS), POOL_GROUP_DIM, POOL_GROUP_DIM)), const((1, POOL_WIDTH)),
        ],
        out_specs=pl.BlockSpec((1, TOK_TILE, d), lambda bi, ti: (bi, ti, 0)),
        out_shape=jax.ShapeDtypeStruct((b, s, d), F32),
        compiler_params=pltpu.CompilerParams(
            dimension_semantics=("arbitrary", "arbitrary"), vmem_limit_bytes=VMEM_LIMIT),
        name="merge",
    )(x, x, oa, ob, gpre, gpost, wu, wg, wa, wb, wc, wout, wpool, pscale)


def _mlp_kernel(x_ref, gpre_ref, gpost_ref, w1_ref, w2_ref, o_ref):
    x = x_ref[0]
    h = _rms(x, gpre_ref[...]).astype(BF16)
    y = None
    for c in range(D_FF // D_MODEL):
        cols = slice(c * D_MODEL, (c + 1) * D_MODEL)
        a = jnp.maximum(_dot(h, w1_ref[:, cols]), 0.0)
        part = _dot((a * a).astype(BF16), w2_ref[cols, :])
        y = part if y is None else y + part
    o_ref[0] = x + _rms(y, gpost_ref[...])


def _mlp(x, gpre, gpost, w1, w2):
    b, s, d = x.shape
    const = lambda shape: pl.BlockSpec(shape, lambda bi, ti: (0,) * len(shape))
    return pl.pallas_call(
        _mlp_kernel,
        grid=(b, s // TOK_TILE),
        in_specs=[pl.BlockSpec((1, TOK_TILE, d), lambda bi, ti: (bi, ti, 0)),
                  const((1, d)), const((1, d)), const((d, D_FF)), const((D_FF, d))],
        out_specs=pl.BlockSpec((1, TOK_TILE, d), lambda bi, ti: (bi, ti, 0)),
        out_shape=jax.ShapeDtypeStruct((b, s, d), F32),
        compiler_params=pltpu.CompilerParams(
            dimension_semantics=("arbitrary", "arbitrary"), vmem_limit_bytes=VMEM_LIMIT),
        name="mlp",
    )(x, gpre, gpost, w1, w2)


def _layer(x, w_in, w_br_a, w_br_b, w_br_c, w_out, w_pool, pool_scale,
           g_pre_mix, g_post_mix, g_pre_mlp, g_post_mlp, w_ff1, w_ff2):
    col = lambda k: slice(k * D_ATT, (k + 1) * D_ATT)
    w_qa, w_ka, w_va = w_in[:, col(0)], w_in[:, col(1)], w_in[:, col(2)]
    wk_hi, wk_lo = _split_bf16(w_ka)
    wqt_hi, wqt_lo = _split_bf16(w_qa.T)
    wvt = w_va.T.astype(BF16)
    wtok = w_in[:, 3 * D_ATT:6 * D_ATT].astype(BF16)
    wu = w_in[:, 6 * D_ATT:6 * D_ATT + POOL_WIDTH].astype(BF16)
    wg = w_in[:, 6 * D_ATT + POOL_WIDTH:].astype(BF16)
    row = lambda v: v.reshape(1, -1)

    ka, qkvb, qt, bias, vt = _in_proj(x, row(g_pre_mix), wk_hi, wk_lo, wtok, wqt_hi, wqt_lo, wvt)
    oa = _moba(qt, bias, ka, vt)
    ob = _dilated(qkvb)
    x = _merge(x, oa, ob, row(g_pre_mix), row(g_post_mix), wu, wg,
               w_br_a.astype(BF16), w_br_b.astype(BF16), w_br_c.astype(BF16), w_out.astype(BF16),
               w_pool.astype(BF16), row(pool_scale))
    return _mlp(x, row(g_pre_mlp), row(g_post_mlp), w_ff1.astype(BF16), w_ff2.astype(BF16))


@jax.jit
def kernel(x, w_in, w_br_a, w_br_b, w_br_c, w_out, w_pool, pool_scale, g_pre_mix, g_post_mix, g_pre_mlp, g_post_mlp, w_ff1, w_ff2):
    for l in range(w_in.shape[0]):
        x = _layer(x, w_in[l], w_br_a[l], w_br_b[l], w_br_c[l], w_out[l], w_pool[l], pool_scale[l],
                   g_pre_mix[l], g_post_mix[l], g_pre_mlp[l], g_post_mlp[l], w_ff1[l], w_ff2[l])
    return x
```

```python
import jax
import jax.numpy as jnp
from jax import lax
from jax.experimental import pallas as pl
from jax.experimental.pallas import tpu as pltpu

F32 = jnp.float32
BF16 = jnp.bfloat16

D_MODEL = 1024
HEAD_DIM = 64
N_HEADS = 8
D_ATT = N_HEADS * HEAD_DIM
PAIR = 2 * HEAD_DIM
N_PAIRS = N_HEADS // 2
MOBA_BLOCK = 256
MOBA_TOPK = 3
MOBA_GROUP = 4
MOBA_QTILE = 512
DENOM_ROWS = 16
VT_ROWS = HEAD_DIM + DENOM_ROWS
LOG2_E = 1.4426950408889634
DIL_PATTERNS = ((128, 1), (512, 4), (2048, 16))
DIL_BAND = 128
DIL_TILE = 2048
POOL_WINDOWS = (2, 4, 8, 16)
POOL_WIDTH = 512
POOL_GROUP_DIM = 128
POOL_HALO = 16
D_FF = 4 * D_MODEL
RMS_EPS = 1e-6
QK_SCALE = HEAD_DIM ** -0.5
NEG = -1e30

TOK_TILE = 512
VMEM_LIMIT = 56 * 1024 * 1024

NT_DIMS = (((1,), (1,)), ((), ()))


def _dot(a, b):
    return jnp.dot(a, b, preferred_element_type=F32)


def _dot_nt(a, b):
    return lax.dot_general(a, b, NT_DIMS, preferred_element_type=F32)


def _split_bf16(v):
    hi = v.astype(BF16)
    lo = (v - hi.astype(F32)).astype(BF16)
    return hi, lo


def _rms(x, g):
    var = jnp.mean(x * x, axis=-1, keepdims=True)
    return x * lax.rsqrt(var + RMS_EPS) * g


def _in_proj_kernel(x_ref, g_ref, wtok_ref, wft_ref, ka_ref, qkvb_ref, qt_ref, bias_ref, vt_ref, kbar_ref):
    t = pl.program_id(1)
    blocks_per_tile = TOK_TILE // MOBA_BLOCK

    @pl.when(t == 0)
    def _():
        kbar_ref[...] = jnp.zeros_like(kbar_ref)

    h = _rms(x_ref[0], g_ref[...]).astype(BF16)

    tok = _dot(h, wtok_ref[...])
    ka = tok[:, 0:D_ATT]
    ka_ref[0] = ka.astype(BF16)
    for blk in range(blocks_per_tile):
        kbar_ref[pl.ds(t * blocks_per_tile + blk, 1), :] = jnp.mean(
            ka[blk * MOBA_BLOCK:(blk + 1) * MOBA_BLOCK], axis=0, keepdims=True)
    qkvb_ref[0, :, 0:D_ATT] = tok[:, D_ATT:2 * D_ATT] * QK_SCALE
    qkvb_ref[0, :, D_ATT:3 * D_ATT] = tok[:, 2 * D_ATT:4 * D_ATT]

    ft = _dot_nt(wft_ref[...], h)
    qt = ft[0:D_ATT] * (QK_SCALE * LOG2_E)
    qt_ref[0] = qt.astype(BF16)
    vt = ft[D_ATT:2 * D_ATT].astype(BF16)
    ones_rows = jnp.ones((DENOM_ROWS, TOK_TILE), BF16)
    for head in range(N_HEADS):
        vt_ref[0, head * VT_ROWS:head * VT_ROWS + HEAD_DIM, :] = vt[head * HEAD_DIM:(head + 1) * HEAD_DIM]
        vt_ref[0, head * VT_ROWS + HEAD_DIM:(head + 1) * VT_ROWS, :] = ones_rows

    half = D_ATT // 2
    r = lax.broadcasted_iota(jnp.int32, (half, half), 0) // HEAD_DIM
    c = lax.broadcasted_iota(jnp.int32, (half, half), 1) // HEAD_DIM
    same_head = r == c
    n_blocks = kbar_ref.shape[0]
    blk_row = lax.broadcasted_iota(jnp.int32, (n_blocks, TOK_TILE), 0).astype(F32)
    q_blk = (t * blocks_per_tile
             + lax.broadcasted_iota(jnp.int32, (n_blocks, TOK_TILE), 1) // MOBA_BLOCK).astype(F32)
    valid = blk_row < q_blk
    for quad in range(2):
        kq = kbar_ref[:, quad * half:(quad + 1) * half]
        kb = jnp.where(same_head, jnp.concatenate([kq] * 4, axis=0), 0.0)
        kb_hi, kb_lo = _split_bf16(kb)
        q_hi, q_lo = _split_bf16(qt[quad * half:(quad + 1) * half])
        gs = _dot(kb_hi, q_hi) + _dot(kb_hi, q_lo) + _dot(kb_lo, q_hi)
        for hq in range(4):
            g = jnp.where(valid, gs[hq * n_blocks:(hq + 1) * n_blocks], -jnp.inf)
            sel = jnp.zeros(g.shape, jnp.bool_)
            for _ in range(MOBA_TOPK):
                m = jnp.max(g, axis=0, keepdims=True)
                idx = jnp.min(jnp.where(g == m, blk_row, float(n_blocks)), axis=0, keepdims=True)
                pick = blk_row == idx
                sel = jnp.logical_or(sel, pick)
                g = jnp.where(pick, -jnp.inf, g)
            head = quad * 4 + hq
            bias_ref[0, head * n_blocks:(head + 1) * n_blocks, :] = jnp.where(
                jnp.logical_and(sel, valid), 0.0, NEG)


def _in_proj(x, g, wtok, wft):
    b, s, d = x.shape
    n_blocks = s // MOBA_BLOCK
    const = lambda shape: pl.BlockSpec(shape, lambda bi, ti: (0,) * len(shape))
    return pl.pallas_call(
        _in_proj_kernel,
        grid=(b, s // TOK_TILE),
        in_specs=[
            pl.BlockSpec((1, TOK_TILE, d), lambda bi, ti: (bi, ti, 0)),
            const((1, d)),
            const((d, 4 * D_ATT)), const((2 * D_ATT, d)),
        ],
        out_specs=[
            pl.BlockSpec((1, TOK_TILE, D_ATT), lambda bi, ti: (bi, ti, 0)),
            pl.BlockSpec((1, TOK_TILE, 3 * D_ATT), lambda bi, ti: (bi, ti, 0)),
            pl.BlockSpec((1, D_ATT, TOK_TILE), lambda bi, ti: (bi, 0, ti)),
            pl.BlockSpec((1, N_HEADS * n_blocks, TOK_TILE), lambda bi, ti: (bi, 0, ti)),
            pl.BlockSpec((1, N_HEADS * VT_ROWS, TOK_TILE), lambda bi, ti: (bi, 0, ti)),
        ],
        out_shape=[
            jax.ShapeDtypeStruct((b, s, D_ATT), BF16),
            jax.ShapeDtypeStruct((b, s, 3 * D_ATT), F32),
            jax.ShapeDtypeStruct((b, D_ATT, s), BF16),
            jax.ShapeDtypeStruct((b, N_HEADS * n_blocks, s), F32),
            jax.ShapeDtypeStruct((b, N_HEADS * VT_ROWS, s), BF16),
        ],
        scratch_shapes=[pltpu.VMEM((n_blocks, D_ATT), F32)],
        compiler_params=pltpu.CompilerParams(
            dimension_semantics=("arbitrary", "arbitrary"), vmem_limit_bytes=VMEM_LIMIT),
        name="in_proj",
    )(x, g, wtok, wft)


def _moba_kernel(qt_ref, bias_ref, k_ref, vt_ref, o_ref, s_sc, p_sc):
    n_blocks = bias_ref.shape[1] // 2
    q_blocks = MOBA_QTILE // MOBA_BLOCK
    trip_blocks = 2 * MOBA_GROUP
    first_blk = pl.program_id(2) * q_blocks
    n_trips = first_blk // trip_blocks
    head_start = n_trips * trip_blocks
    qt = qt_ref[0]
    zeros = jnp.zeros((HEAD_DIM, MOBA_QTILE), BF16)
    qt_heads = (jnp.concatenate([qt[:HEAD_DIM], zeros], axis=0),
                jnp.concatenate([zeros, qt[HEAD_DIM:]], axis=0))
    key_pos = lax.broadcasted_iota(jnp.int32, (MOBA_BLOCK, MOBA_QTILE), 0)
    qry_pos = lax.broadcasted_iota(jnp.int32, (MOBA_BLOCK, MOBA_QTILE), 1)
    future = key_pos > qry_pos % MOBA_BLOCK
    qry_blk = qry_pos // MOBA_BLOCK
    qry_blk_row = qry_blk[0:1]

    def owner(r, nb, own_tail):
        return r - (nb - q_blocks) if own_tail and r >= nb - q_blocks else None

    def key_rows(start_blk, nb):
        return pl.ds(pl.multiple_of(start_blk * MOBA_BLOCK, q_blocks * MOBA_BLOCK), nb * MOBA_BLOCK)

    def scores(start_blk, nb, slot, own_tail=False):
        k_c = k_ref[0, key_rows(start_blk, nb), :]
        col_max = []
        for hd in range(2):
            s = _dot(k_c, qt_heads[hd])
            for r in range(nb):
                s_r = s[r * MOBA_BLOCK:(r + 1) * MOBA_BLOCK]
                if owner(r, nb, own_tail) is not None:
                    s_r = s_r + jnp.where(jnp.logical_and(qry_blk == owner(r, nb, own_tail), future), NEG, 0.0)
                s_sc[slot, hd, r * MOBA_BLOCK:(r + 1) * MOBA_BLOCK, :] = s_r
                col_max.append(jnp.max(s_r, axis=0, keepdims=True))
        return tuple(col_max)

    def softmax(start_blk, nb, slot, m_pair, col_max, own_tail=False):
        alphas, m_news = [], []
        for hd in range(2):
            gates = []
            for r in range(nb):
                gate = bias_ref[0, pl.ds(hd * n_blocks + start_blk + r, 1), :]
                if owner(r, nb, own_tail) is not None:
                    gate = jnp.where(qry_blk_row == owner(r, nb, own_tail), 0.0, gate)
                gates.append(gate)
            m_new = m_pair[hd]
            for r, gate in enumerate(gates):
                m_new = jnp.maximum(m_new, col_max[hd * nb + r] + gate)
            alphas.append(jnp.exp2(m_pair[hd] - m_new))
            m_news.append(m_new)
            for r, gate in enumerate(gates):
                rows = slice(r * MOBA_BLOCK, (r + 1) * MOBA_BLOCK)
                p_sc[slot, hd, rows, :] = jnp.exp2(s_sc[slot, hd, rows, :] - (m_new - gate)).astype(BF16)
        return tuple(alphas), tuple(m_news)

    def accum(start_blk, slot, alphas, accs):
        vt_c = vt_ref[0, :, key_rows(start_blk, MOBA_GROUP)]
        return tuple(alphas[hd] * accs[hd] + _dot(vt_c[hd * VT_ROWS:(hd + 1) * VT_ROWS], p_sc[slot, hd])
                     for hd in range(2))

    m0 = (jnp.full((1, MOBA_QTILE), NEG, F32),) * 2
    acc0 = (jnp.zeros((VT_ROWS, MOBA_QTILE), F32),) * 2

    def head_part(n_rem):
        def run():
            total = n_rem + q_blocks
            sizes = [total] if total <= MOBA_GROUP else [total - MOBA_GROUP, MOBA_GROUP]
            m, acc, pending, start = m0, acc0, None, head_start
            for c, nb in enumerate(sizes):
                slot = len(sizes) - 1 - c
                last = c == len(sizes) - 1
                cm = scores(start, nb, slot, own_tail=last)
                alpha, m = softmax(start, nb, slot, m, cm, own_tail=last)
                if pending is not None:
                    acc = accum(pending[0], 1, pending[1], acc)
                if nb < MOBA_GROUP:
                    for hd in range(2):
                        p_sc[slot, hd, nb * MOBA_BLOCK:, :] = jnp.zeros(
                            ((MOBA_GROUP - nb) * MOBA_BLOCK, MOBA_QTILE), BF16)
                pending = (start, alpha)
                start = start + nb
            cm_first = scores(0, MOBA_GROUP, 1)
            return pending[1], m, acc, cm_first
        return run

    variants = [head_part(n_rem) for n_rem in range(0, trip_blocks, q_blocks)]
    n_rem = first_blk - head_start
    alpha_h, m_h, acc_h, cm_first = lax.switch(n_rem // q_blocks, variants)
    pending_start = jnp.where(n_rem + q_blocks <= MOBA_GROUP, head_start, first_blk + q_blocks - MOBA_GROUP)

    def trip(t, carry):
        alpha_prev, m, acc, cm_a = carry
        blk_a = t * trip_blocks
        blk_b = blk_a + MOBA_GROUP
        cm_b = scores(blk_b, MOBA_GROUP, 0)
        alpha_a, m = softmax(blk_a, MOBA_GROUP, 1, m, cm_a)
        acc = accum(jnp.where(t == 0, pending_start, blk_a - MOBA_GROUP), 0, alpha_prev, acc)
        cm_next = scores(jnp.minimum(blk_a + trip_blocks, n_blocks - MOBA_GROUP), MOBA_GROUP, 1)
        alpha_b, m = softmax(blk_b, MOBA_GROUP, 0, m, cm_b)
        acc = accum(blk_a, 1, alpha_a, acc)
        return alpha_b, m, acc, cm_next

    alpha_l, _, acc, _ = lax.fori_loop(0, n_trips, trip, (alpha_h, m_h, acc_h, cm_first))
    acc = accum(jnp.where(n_trips == 0, pending_start, head_start - MOBA_GROUP), 0, alpha_l, acc)
    ot = jnp.concatenate([a[:HEAD_DIM] * (1.0 / a[HEAD_DIM:HEAD_DIM + 1]) for a in acc], axis=0)
    o_ref[0] = ot.T.astype(BF16)


def _moba(qt, bias, ka, vt):
    b, s, _ = ka.shape
    n_blocks = s // MOBA_BLOCK
    return pl.pallas_call(
        _moba_kernel,
        grid=(b, N_PAIRS, s // MOBA_QTILE),
        in_specs=[
            pl.BlockSpec((1, PAIR, MOBA_QTILE), lambda bi, p, i: (bi, p, i)),
            pl.BlockSpec((1, 2 * n_blocks, MOBA_QTILE), lambda bi, p, i: (bi, p, i)),
            pl.BlockSpec((1, s, PAIR), lambda bi, p, i: (bi, 0, p)),
            pl.BlockSpec((1, 2 * VT_ROWS, s), lambda bi, p, i: (bi, p, 0)),
        ],
        out_specs=pl.BlockSpec((1, MOBA_QTILE, PAIR), lambda bi, p, i: (bi, i, p)),
        out_shape=jax.ShapeDtypeStruct((b, s, D_ATT), BF16),
        scratch_shapes=[pltpu.VMEM((2, 2, MOBA_GROUP * MOBA_BLOCK, MOBA_QTILE), F32),
                        pltpu.VMEM((2, 2, MOBA_GROUP * MOBA_BLOCK, MOBA_QTILE), BF16)],
        compiler_params=pltpu.CompilerParams(
            dimension_semantics=("arbitrary", "arbitrary", "arbitrary"), vmem_limit_bytes=VMEM_LIMIT),
        name="moba",
    )(qt, bias, ka, vt)


def _dilated_kernel(q_ref, kc_ref, kp_ref, vc_ref, vp_ref, o_ref, o_sc, lse_sc):
    first_tile = pl.program_id(2) == 0
    n_tiles = DIL_TILE // DIL_BAND
    head_a = lax.broadcasted_iota(jnp.int32, (DIL_TILE, PAIR), 1) < HEAD_DIM
    qi = lax.broadcasted_iota(jnp.int32, (DIL_BAND, 2 * DIL_BAND), 0)
    kj = lax.broadcasted_iota(jnp.int32, (DIL_BAND, 2 * DIL_BAND), 1)
    band = jnp.where(jnp.logical_and(kj >= qi, kj <= qi + DIL_BAND), 0.0, NEG)
    band_start = jnp.where(first_tile, jnp.where(kj >= DIL_BAND, band, NEG), band)

    for pat, (window, dil) in enumerate(DIL_PATTERNS):
        rows = DIL_TILE // dil
        n_q = rows // DIL_BAND
        strided = lambda ref, start, size: ref[0, pl.ds(start, size, stride=dil), :]
        q_parts, k_tiles, v_tiles, masks = [], [], [], []
        for r in range(dil):
            q_parts.append(strided(q_ref, r, rows))
            prev_start = DIL_TILE - DIL_BAND * dil + r
            k_ext = jnp.concatenate([strided(kp_ref, prev_start, DIL_BAND), strided(kc_ref, r, rows)],
                                    axis=0).astype(BF16)
            v_ext = jnp.concatenate([strided(vp_ref, prev_start, DIL_BAND), strided(vc_ref, r, rows)],
                                    axis=0).astype(BF16)
            for qb in range(n_q):
                k_tiles.append(k_ext[qb * DIL_BAND:(qb + 2) * DIL_BAND])
                v_tiles.append(v_ext[qb * DIL_BAND:(qb + 2) * DIL_BAND])
                masks.append(band_start if qb == 0 else band)
        q_all = jnp.concatenate(q_parts, axis=0)
        mask_all = jnp.concatenate(masks, axis=0)
        outs, lses = [], []
        for hd in range(2):
            qh = jnp.where(head_a if hd == 0 else jnp.logical_not(head_a), q_all, 0.0).astype(BF16)
            s = jnp.concatenate(
                [_dot_nt(qh[n * DIL_BAND:(n + 1) * DIL_BAND], k_tiles[n]) for n in range(n_tiles)],
                axis=0) + mask_all
            m = jnp.max(s, axis=-1, keepdims=True)
            p = jnp.exp(s - m)
            l = jnp.sum(p, axis=-1, keepdims=True)
            pb = p.astype(BF16)
            o = jnp.concatenate(
                [_dot(pb[n * DIL_BAND:(n + 1) * DIL_BAND], v_tiles[n]) for n in range(n_tiles)], axis=0)
            outs.append(o / l)
            lses.append(m + jnp.log(l))
        o_pair = jnp.where(head_a, outs[0], outs[1])
        lse_pair = jnp.where(head_a, lses[0], lses[1])
        for r in range(dil):
            o_sc[pat, pl.ds(r, rows, stride=dil), :] = o_pair[r * rows:(r + 1) * rows]
            lse_sc[pat, pl.ds(r, rows, stride=dil), :] = lse_pair[r * rows:(r + 1) * rows]

    lse = lse_sc[...]
    w = jnp.exp(lse - jnp.max(lse, axis=0, keepdims=True))
    o_ref[0] = (jnp.sum(w * o_sc[...], axis=0) / jnp.sum(w, axis=0)).astype(BF16)


def _dilated(qkvb):
    b, s, _ = qkvb.shape
    tile_spec = lambda col0, prev: pl.BlockSpec(
        (1, DIL_TILE, PAIR),
        (lambda bi, p, t: (bi, jnp.maximum(t - 1, 0), col0 + p)) if prev else (lambda bi, p, t: (bi, t, col0 + p)))
    return pl.pallas_call(
        _dilated_kernel,
        grid=(b, N_PAIRS, s // DIL_TILE),
        in_specs=[tile_spec(0, False),
                  tile_spec(N_PAIRS, False), tile_spec(N_PAIRS, True),
                  tile_spec(2 * N_PAIRS, False), tile_spec(2 * N_PAIRS, True)],
        out_specs=pl.BlockSpec((1, DIL_TILE, PAIR), lambda bi, p, t: (bi, t, p)),
        out_shape=jax.ShapeDtypeStruct((b, s, D_ATT), BF16),
        scratch_shapes=[pltpu.VMEM((len(DIL_PATTERNS), DIL_TILE, PAIR), F32),
                        pltpu.VMEM((len(DIL_PATTERNS), DIL_TILE, PAIR), F32)],
        compiler_params=pltpu.CompilerParams(
            dimension_semantics=("arbitrary", "arbitrary", "arbitrary"), vmem_limit_bytes=VMEM_LIMIT),
        name="dilated",
    )(qkvb, qkvb, qkvb, qkvb, qkvb)


def _merge_kernel(x_ref, xh_ref, oa_ref, ob_ref, gpre_ref, gpost_ref, wu_ref, wg_ref, wa_ref, wb_ref, wc_ref,
                  wout_ref, wpool_ref, pscale_ref, o_ref):
    t = pl.program_id(1)
    x = x_ref[0]
    h = _rms(x, gpre_ref[...]).astype(BF16)
    hh = _rms(xh_ref[0], gpre_ref[...]).astype(BF16)
    u = _dot(h, wu_ref[...])
    uh = jnp.where(t == 0, 0.0, _dot(hh, wu_ref[...]))
    u_ext = jnp.concatenate([uh, u], axis=0)
    pos = (t * TOK_TILE + lax.broadcasted_iota(jnp.int32, (TOK_TILE, POOL_GROUP_DIM), 0) + 1).astype(F32)
    oc = []
    for g, w in enumerate(POOL_WINDOWS):
        cols = slice(g * POOL_GROUP_DIM, (g + 1) * POOL_GROUP_DIM)
        acc = u_ext[:, cols]
        span = 1
        while span < w:
            acc = acc + pltpu.roll(acc, span, 0)
            span *= 2
        pooled = acc[POOL_HALO:] / jnp.minimum(pos, float(w)) - u[:, cols]
        oc.append(_dot(pooled.astype(BF16), wpool_ref[g]))
    oc = (jnp.concatenate(oc, axis=1) * pscale_ref[...]).astype(BF16)

    merged = None
    for br, (src, w_ref) in enumerate(((oa_ref[0], wa_ref), (ob_ref[0], wb_ref), (oc, wc_ref))):
        gz = _dot(h, wg_ref[:, br * D_MODEL:(br + 1) * D_MODEL])
        term = (1.0 / (1.0 + jnp.exp(-gz))) * _dot(src, w_ref[...])
        merged = term if merged is None else merged + term
    y = _dot(merged.astype(BF16), wout_ref[...])
    o_ref[0] = x + _rms(y, gpost_ref[...])


def _merge(x, oa, ob, gpre, gpost, wu, wg, wa, wb, wc, wout, wpool, pscale):
    b, s, d = x.shape
    const = lambda shape: pl.BlockSpec(shape, lambda bi, ti: (0,) * len(shape))
    halo_blocks = TOK_TILE // POOL_HALO
    return pl.pallas_call(
        _merge_kernel,
        grid=(b, s // TOK_TILE),
        in_specs=[
            pl.BlockSpec((1, TOK_TILE, d), lambda bi, ti: (bi, ti, 0)),
            pl.BlockSpec((1, POOL_HALO, d), lambda bi, ti: (bi, jnp.maximum(ti * halo_blocks - 1, 0), 0)),
            pl.BlockSpec((1, TOK_TILE, D_ATT), lambda bi, ti: (bi, ti, 0)),
            pl.BlockSpec((1, TOK_TILE, D_ATT), lambda bi, ti: (bi, ti, 0)),
            const((1, d)), const((1, d)),
            const((d, POOL_WIDTH)), const((d, 3 * d)),
            const((D_ATT, d)), const((D_ATT, d)), const((POOL_WIDTH, d)),
            const((d, d)), const((len(POOL_WINDOWS), POOL_GROUP_DIM, POOL_GROUP_DIM)), const((1, POOL_WIDTH)),
        ],
        out_specs=pl.BlockSpec((1, TOK_TILE, d), lambda bi, ti: (bi, ti, 0)),
        out_shape=jax.ShapeDtypeStruct((b, s, d), F32),
        compiler_params=pltpu.CompilerParams(
            dimension_semantics=("arbitrary", "arbitrary"), vmem_limit_bytes=VMEM_LIMIT),
        name="merge",
    )(x, x, oa, ob, gpre, gpost, wu, wg, wa, wb, wc, wout, wpool, pscale)


def _mlp_kernel(x_ref, gpre_ref, gpost_ref, w1_ref, w2_ref, o_ref):
    x = x_ref[0]
    h = _rms(x, gpre_ref[...]).astype(BF16)
    y = None
    for c in range(D_FF // D_MODEL):
        cols = slice(c * D_MODEL, (c + 1) * D_MODEL)
        a = jnp.maximum(_dot(h, w1_ref[:, cols]), 0.0)
        part = _dot((a * a).astype(BF16), w2_ref[cols, :])
        y = part if y is None else y + part
    o_ref[0] = x + _rms(y, gpost_ref[...])


def _mlp(x, gpre, gpost, w1, w2):
    b, s, d = x.shape
    const = lambda shape: pl.BlockSpec(shape, lambda bi, ti: (0,) * len(shape))
    return pl.pallas_call(
        _mlp_kernel,
        grid=(b, s // TOK_TILE),
        in_specs=[pl.BlockSpec((1, TOK_TILE, d), lambda bi, ti: (bi, ti, 0)),
                  const((1, d)), const((1, d)), const((d, D_FF)), const((D_FF, d))],
        out_specs=pl.BlockSpec((1, TOK_TILE, d), lambda bi, ti: (bi, ti, 0)),
        out_shape=jax.ShapeDtypeStruct((b, s, d), F32),
        compiler_params=pltpu.CompilerParams(
            dimension_semantics=("arbitrary", "arbitrary"), vmem_limit_bytes=VMEM_LIMIT),
        name="mlp",
    )(x, gpre, gpost, w1, w2)


def _layer(x, w_in, w_br_a, w_br_b, w_br_c, w_out, w_pool, pool_scale,
           g_pre_mix, g_post_mix, g_pre_mlp, g_post_mlp, w_ff1, w_ff2):
    col = lambda k: slice(k * D_ATT, (k + 1) * D_ATT)
    wtok = w_in[:, D_ATT:6 * D_ATT]
    wtok = jnp.concatenate([wtok[:, :D_ATT], wtok[:, 2 * D_ATT:]], axis=1).astype(BF16)
    wft = jnp.concatenate([w_in[:, col(0)], w_in[:, col(2)]], axis=1).T.astype(BF16)
    wu = w_in[:, 6 * D_ATT:6 * D_ATT + POOL_WIDTH].astype(BF16)
    wg = w_in[:, 6 * D_ATT + POOL_WIDTH:].astype(BF16)
    row = lambda v: v.reshape(1, -1)

    ka, qkvb, qt, bias, vt = _in_proj(x, row(g_pre_mix), wtok, wft)
    oa = _moba(qt, bias, ka, vt)
    ob = _dilated(qkvb)
    x = _merge(x, oa, ob, row(g_pre_mix), row(g_post_mix), wu, wg,
               w_br_a.astype(BF16), w_br_b.astype(BF16), w_br_c.astype(BF16), w_out.astype(BF16),
               w_pool.astype(BF16), row(pool_scale))
    return _mlp(x, row(g_pre_mlp), row(g_post_mlp), w_ff1.astype(BF16), w_ff2.astype(BF16))


@jax.jit
def kernel(x, w_in, w_br_a, w_br_b, w_br_c, w_out, w_pool, pool_scale, g_pre_mix, g_post_mix, g_pre_mlp, g_post_mlp, w_ff1, w_ff2):
    for l in range(w_in.shape[0]):
        x = _layer(x, w_in[l], w_br_a[l], w_br_b[l], w_br_c[l], w_out[l], w_pool[l], pool_scale[l],
                   g_pre_mix[l], g_post_mix[l], g_pre_mlp[l], g_post_mlp[l], w_ff1[l], w_ff2[l])
    return x
```

```python
import jax
import jax.numpy as jnp
import numpy as np
from jax import lax
from jax.experimental import pallas as pl
from jax.experimental.pallas import tpu as pltpu

F32 = jnp.float32
BF16 = jnp.bfloat16

D_MODEL = 1024
HEAD_DIM = 64
N_HEADS = 8
D_ATT = N_HEADS * HEAD_DIM
PAIR = 2 * HEAD_DIM
N_PAIRS = N_HEADS // 2
MOBA_BLOCK = 256
MOBA_TOPK = 3
MOBA_GROUP = 4
MOBA_QTILE = 512
DENOM_ROWS = 16
VT_ROWS = HEAD_DIM + DENOM_ROWS
LOG2_E = 1.4426950408889634
DIL_PATTERNS = ((128, 1), (512, 4), (2048, 16))
DIL_BAND = 128
DIL_TILE = 2048
POOL_WINDOWS = (2, 4, 8, 16)
POOL_WIDTH = 512
POOL_GROUP_DIM = 128
POOL_HALO = 16
D_FF = 4 * D_MODEL
RMS_EPS = 1e-6
QK_SCALE = HEAD_DIM ** -0.5
NEG = -1e30

TOK_TILE = 512
VMEM_LIMIT = 56 * 1024 * 1024

NT_DIMS = (((1,), (1,)), ((), ()))


def _dot(a, b):
    return jnp.dot(a, b, preferred_element_type=F32)


def _dot_nt(a, b):
    return lax.dot_general(a, b, NT_DIMS, preferred_element_type=F32)


def _split_bf16(v):
    hi = v.astype(BF16)
    lo = (v - hi.astype(F32)).astype(BF16)
    return hi, lo


def _rms(x, g):
    var = jnp.mean(x * x, axis=-1, keepdims=True)
    return x * lax.rsqrt(var + RMS_EPS) * g


def _in_proj_kernel(x_ref, g_ref, wtok_ref, wft_ref, ka_ref, qkvb_ref, qt_ref, bias_ref, vt_ref, kbar_ref):
    t = pl.program_id(1)
    blocks_per_tile = TOK_TILE // MOBA_BLOCK

    @pl.when(t == 0)
    def _():
        kbar_ref[...] = jnp.zeros_like(kbar_ref)

    h = _rms(x_ref[0], g_ref[...]).astype(BF16)

    tok = _dot(h, wtok_ref[...])
    ka = tok[:, 0:D_ATT]
    ka_ref[0] = ka.astype(BF16)
    for blk in range(blocks_per_tile):
        kbar_ref[pl.ds(t * blocks_per_tile + blk, 1), :] = jnp.mean(
            ka[blk * MOBA_BLOCK:(blk + 1) * MOBA_BLOCK], axis=0, keepdims=True)
    qkvb_ref[0, :, 0:D_ATT] = tok[:, D_ATT:2 * D_ATT] * QK_SCALE
    qkvb_ref[0, :, D_ATT:3 * D_ATT] = tok[:, 2 * D_ATT:4 * D_ATT]

    ft = _dot_nt(wft_ref[...], h)
    qt = ft[0:D_ATT] * (QK_SCALE * LOG2_E)
    qt_ref[0] = qt.astype(BF16)
    vt = ft[D_ATT:2 * D_ATT].astype(BF16)
    ones_rows = jnp.ones((DENOM_ROWS, TOK_TILE), BF16)
    for head in range(N_HEADS):
        vt_ref[0, head * VT_ROWS:head * VT_ROWS + HEAD_DIM, :] = vt[head * HEAD_DIM:(head + 1) * HEAD_DIM]
        vt_ref[0, head * VT_ROWS + HEAD_DIM:(head + 1) * VT_ROWS, :] = ones_rows

    half = D_ATT // 2
    r = lax.broadcasted_iota(jnp.int32, (half, half), 0) // HEAD_DIM
    c = lax.broadcasted_iota(jnp.int32, (half, half), 1) // HEAD_DIM
    same_head = r == c
    n_blocks = kbar_ref.shape[0]
    blk_row = lax.broadcasted_iota(jnp.int32, (n_blocks, TOK_TILE), 0).astype(F32)
    q_blk = (t * blocks_per_tile
             + lax.broadcasted_iota(jnp.int32, (n_blocks, TOK_TILE), 1) // MOBA_BLOCK).astype(F32)
    valid = blk_row < q_blk
    for quad in range(2):
        kq = kbar_ref[:, quad * half:(quad + 1) * half]
        kb = jnp.where(same_head, jnp.concatenate([kq] * 4, axis=0), 0.0)
        kb_hi, kb_lo = _split_bf16(kb)
        q_hi, q_lo = _split_bf16(qt[quad * half:(quad + 1) * half])
        gs = _dot(kb_hi, q_hi) + _dot(kb_hi, q_lo) + _dot(kb_lo, q_hi)
        for hq in range(4):
            g = jnp.where(valid, gs[hq * n_blocks:(hq + 1) * n_blocks], -jnp.inf)
            sel = jnp.zeros(g.shape, jnp.bool_)
            for _ in range(MOBA_TOPK):
                m = jnp.max(g, axis=0, keepdims=True)
                idx = jnp.min(jnp.where(g == m, blk_row, float(n_blocks)), axis=0, keepdims=True)
                pick = blk_row == idx
                sel = jnp.logical_or(sel, pick)
                g = jnp.where(pick, -jnp.inf, g)
            head = quad * 4 + hq
            bias_ref[0, head * n_blocks:(head + 1) * n_blocks, :] = jnp.where(
                jnp.logical_and(sel, valid), 0.0, NEG)


def _in_proj(x, g, wtok, wft):
    b, s, d = x.shape
    n_blocks = s // MOBA_BLOCK
    const = lambda shape: pl.BlockSpec(shape, lambda bi, ti: (0,) * len(shape))
    return pl.pallas_call(
        _in_proj_kernel,
        grid=(b, s // TOK_TILE),
        in_specs=[
            pl.BlockSpec((1, TOK_TILE, d), lambda bi, ti: (bi, ti, 0)),
            const((1, d)),
            const((d, 4 * D_ATT)), const((2 * D_ATT, d)),
        ],
        out_specs=[
            pl.BlockSpec((1, TOK_TILE, D_ATT), lambda bi, ti: (bi, ti, 0)),
            pl.BlockSpec((1, TOK_TILE, 3 * D_ATT), lambda bi, ti: (bi, ti, 0)),
            pl.BlockSpec((1, D_ATT, TOK_TILE), lambda bi, ti: (bi, 0, ti)),
            pl.BlockSpec((1, N_HEADS * n_blocks, TOK_TILE), lambda bi, ti: (bi, 0, ti)),
            pl.BlockSpec((1, N_HEADS * VT_ROWS, TOK_TILE), lambda bi, ti: (bi, 0, ti)),
        ],
        out_shape=[
            jax.ShapeDtypeStruct((b, s, D_ATT), BF16),
            jax.ShapeDtypeStruct((b, s, 3 * D_ATT), F32),
            jax.ShapeDtypeStruct((b, D_ATT, s), BF16),
            jax.ShapeDtypeStruct((b, N_HEADS * n_blocks, s), F32),
            jax.ShapeDtypeStruct((b, N_HEADS * VT_ROWS, s), BF16),
        ],
        scratch_shapes=[pltpu.VMEM((n_blocks, D_ATT), F32)],
        compiler_params=pltpu.CompilerParams(
            dimension_semantics=("arbitrary", "arbitrary"), vmem_limit_bytes=VMEM_LIMIT),
        name="in_proj",
    )(x, g, wtok, wft)


def _moba_steps(n_blocks):
    q_blocks = MOBA_QTILE // MOBA_BLOCK
    steps = [(0, 0, 1, 0)]
    for tile in range(1, n_blocks // q_blocks):
        n_past = tile * q_blocks - q_blocks
        steps.append((tile, n_past, 1, 0))
        for start in range(0, n_past, MOBA_GROUP):
            steps.append((tile, start, 0, 1 if start + MOBA_GROUP > n_past else 0))
    if len(steps) % 2 == 0:
        steps.append((steps[-1][0], steps[-1][1], 0, 2))
    return [np.asarray(col, np.int32) for col in zip(*steps)]


def _moba_kernel(tile_tab, start_tab, first_tab, off_tab, qt_ref, bias_ref, k_ref, vt_ref, o_ref, s_sc, p_sc):
    n_blocks = bias_ref.shape[1] // 2
    n_steps = tile_tab.shape[0]
    q_blocks = MOBA_QTILE // MOBA_BLOCK
    chunk_keys = MOBA_GROUP * MOBA_BLOCK
    key_pos = lax.broadcasted_iota(jnp.int32, (MOBA_BLOCK, MOBA_QTILE), 0)
    qry_pos = lax.broadcasted_iota(jnp.int32, (MOBA_BLOCK, MOBA_QTILE), 1)
    future = key_pos > qry_pos % MOBA_BLOCK
    qry_blk = qry_pos // MOBA_BLOCK
    qry_blk_row = qry_blk[0:1]
    zeros = jnp.zeros((HEAD_DIM, MOBA_QTILE), BF16)
    tail = tuple(range(MOBA_GROUP - q_blocks, MOBA_GROUP))

    def tile_lanes(n):
        return pl.ds(pl.multiple_of(tile_tab[n] * MOBA_QTILE, MOBA_QTILE), MOBA_QTILE)

    def chunk_keys_of(n):
        return pl.ds(pl.multiple_of(start_tab[n] * MOBA_BLOCK, q_blocks * MOBA_BLOCK), chunk_keys)

    def scores(n, slot, own_pos=tail):
        qt = qt_ref[0, :, tile_lanes(n)]
        qt_heads = (jnp.concatenate([qt[:HEAD_DIM], zeros], axis=0),
                    jnp.concatenate([zeros, qt[HEAD_DIM:]], axis=0))
        k_c = k_ref[0, chunk_keys_of(n), :]
        first = first_tab[n] == 1
        col_max = []
        for hd in range(2):
            s = _dot(k_c, qt_heads[hd])
            for r in range(MOBA_GROUP):
                s_r = s[r * MOBA_BLOCK:(r + 1) * MOBA_BLOCK]
                if r in own_pos:
                    own = jnp.logical_and(first, jnp.logical_and(qry_blk == own_pos.index(r), future))
                    s_r = s_r + jnp.where(own, NEG, 0.0)
                s_sc[slot, hd, r * MOBA_BLOCK:(r + 1) * MOBA_BLOCK, :] = s_r
                col_max.append(jnp.max(s_r, axis=0, keepdims=True))
        return tuple(col_max)

    def softmax(n, slot, m_pair, col_max, own_pos=tail):
        first = first_tab[n] == 1
        off = off_tab[n]
        alphas, m_news = [], []
        for hd in range(2):
            m_old = jnp.where(first, NEG, m_pair[hd])
            gates = []
            for r in range(MOBA_GROUP):
                gate = bias_ref[0, pl.ds(hd * n_blocks + start_tab[n] + r, 1), tile_lanes(n)]
                if r in own_pos:
                    gate = jnp.where(jnp.logical_and(first, qry_blk_row == own_pos.index(r)), 0.0, gate)
                gated_off = off >= 1 if r >= MOBA_GROUP - q_blocks else off >= 2
                gates.append(gate + jnp.where(gated_off, NEG, 0.0))
            m_new = m_old
            for r, gate in enumerate(gates):
                m_new = jnp.maximum(m_new, col_max[hd * MOBA_GROUP + r] + gate)
            alphas.append(jnp.exp2(m_old - m_new))
            m_news.append(m_new)
            for r, gate in enumerate(gates):
                rows = slice(r * MOBA_BLOCK, (r + 1) * MOBA_BLOCK)
                p_sc[slot, hd, rows, :] = jnp.exp2(s_sc[slot, hd, rows, :] - (m_new - gate)).astype(BF16)
        return tuple(alphas), tuple(m_news)

    def store(n, accs):
        ot = jnp.concatenate([a[:HEAD_DIM] * (1.0 / a[HEAD_DIM:HEAD_DIM + 1]) for a in accs], axis=0)
        o_ref[0, tile_lanes(n), :] = ot.T.astype(BF16)

    def accum(n, slot, alphas, accs):
        store(jnp.maximum(n - 1, 0), accs)
        vt_c = vt_ref[0, :, chunk_keys_of(n)]
        return tuple(alphas[hd] * accs[hd] + _dot(vt_c[hd * VT_ROWS:(hd + 1) * VT_ROWS], p_sc[slot, hd])
                     for hd in range(2))

    m0 = (jnp.full((1, MOBA_QTILE), NEG, F32),) * 2
    acc0 = (jnp.ones((VT_ROWS, MOBA_QTILE), F32),) * 2
    head = tuple(range(q_blocks))
    alpha0, m1 = softmax(0, 0, m0, scores(0, 0, own_pos=head), own_pos=head)
    cm1 = scores(1, 1)

    def two_steps(t, carry):
        alpha_prev, m, acc, cm_a = carry
        n_a = 2 * t + 1
        n_b = n_a + 1
        cm_b = scores(n_b, 0)
        alpha_a, m = softmax(n_a, 1, m, cm_a)
        acc = accum(n_a - 1, 0, alpha_prev, acc)
        cm_next = scores(jnp.minimum(n_b + 1, n_steps - 1), 1)
        alpha_b, m = softmax(n_b, 0, m, cm_b)
        acc = accum(n_a, 1, alpha_a, acc)
        return alpha_b, m, acc, cm_next

    alpha_l, _, acc, _ = lax.fori_loop(0, (n_steps - 1) // 2, two_steps, (alpha0, m1, acc0, cm1))
    store(n_steps - 1, accum(n_steps - 1, 0, alpha_l, acc))


def _moba(qt, bias, ka, vt):
    b, s, _ = ka.shape
    n_blocks = s // MOBA_BLOCK
    tables = _moba_steps(n_blocks)
    whole = lambda shape, index: pl.BlockSpec(shape, index, pipeline_mode=pl.Buffered(1))
    return pl.pallas_call(
        _moba_kernel,
        grid_spec=pltpu.PrefetchScalarGridSpec(
            num_scalar_prefetch=len(tables),
            grid=(b, N_PAIRS),
            in_specs=[
                whole((1, PAIR, s), lambda bi, p, *_: (bi, p, 0)),
                whole((1, 2 * n_blocks, s), lambda bi, p, *_: (bi, p, 0)),
                whole((1, s, PAIR), lambda bi, p, *_: (bi, 0, p)),
                whole((1, 2 * VT_ROWS, s), lambda bi, p, *_: (bi, p, 0)),
            ],
            out_specs=pl.BlockSpec((1, s, PAIR), lambda bi, p, *_: (bi, 0, p)),
            scratch_shapes=[pltpu.VMEM((2, 2, MOBA_GROUP * MOBA_BLOCK, MOBA_QTILE), F32),
                            pltpu.VMEM((2, 2, MOBA_GROUP * MOBA_BLOCK, MOBA_QTILE), BF16)],
        ),
        out_shape=jax.ShapeDtypeStruct((b, s, D_ATT), BF16),
        compiler_params=pltpu.CompilerParams(
            dimension_semantics=("arbitrary", "arbitrary"), vmem_limit_bytes=VMEM_LIMIT),
        name="moba",
    )(*tables, qt, bias, ka, vt)


def _dilated_kernel(q_ref, kc_ref, kp_ref, vc_ref, vp_ref, o_ref, o_sc, lse_sc):
    first_tile = pl.program_id(2) == 0
    n_tiles = DIL_TILE // DIL_BAND
    head_a = lax.broadcasted_iota(jnp.int32, (DIL_TILE, PAIR), 1) < HEAD_DIM
    qi = lax.broadcasted_iota(jnp.int32, (DIL_BAND, 2 * DIL_BAND), 0)
    kj = lax.broadcasted_iota(jnp.int32, (DIL_BAND, 2 * DIL_BAND), 1)
    band = jnp.where(jnp.logical_and(kj >= qi, kj <= qi + DIL_BAND), 0.0, NEG)
    band_start = jnp.where(first_tile, jnp.where(kj >= DIL_BAND, band, NEG), band)

    for pat, (window, dil) in enumerate(DIL_PATTERNS):
        rows = DIL_TILE // dil
        n_q = rows // DIL_BAND
        strided = lambda ref, start, size: ref[0, pl.ds(start, size, stride=dil), :]
        q_parts, k_tiles, v_tiles, masks = [], [], [], []
        for r in range(dil):
            q_parts.append(strided(q_ref, r, rows))
            prev_start = DIL_TILE - DIL_BAND * dil + r
            k_ext = jnp.concatenate([strided(kp_ref, prev_start, DIL_BAND), strided(kc_ref, r, rows)],
                                    axis=0).astype(BF16)
            v_ext = jnp.concatenate([strided(vp_ref, prev_start, DIL_BAND), strided(vc_ref, r, rows)],
                                    axis=0).astype(BF16)
            for qb in range(n_q):
                k_tiles.append(k_ext[qb * DIL_BAND:(qb + 2) * DIL_BAND])
                v_tiles.append(v_ext[qb * DIL_BAND:(qb + 2) * DIL_BAND])
                masks.append(band_start if qb == 0 else band)
        q_all = jnp.concatenate(q_parts, axis=0)
        mask_all = jnp.concatenate(masks, axis=0)
        outs, lses = [], []
        for hd in range(2):
            qh = jnp.where(head_a if hd == 0 else jnp.logical_not(head_a), q_all, 0.0).astype(BF16)
            s = jnp.concatenate(
                [_dot_nt(qh[n * DIL_BAND:(n + 1) * DIL_BAND], k_tiles[n]) for n in range(n_tiles)],
                axis=0) + mask_all
            m = jnp.max(s, axis=-1, keepdims=True)
            p = jnp.exp(s - m)
            l = jnp.sum(p, axis=-1, keepdims=True)
            pb = p.astype(BF16)
            o = jnp.concatenate(
                [_dot(pb[n * DIL_BAND:(n + 1) * DIL_BAND], v_tiles[n]) for n in range(n_tiles)], axis=0)
            outs.append(o / l)
            lses.append(m + jnp.log(l))
        o_pair = jnp.where(head_a, outs[0], outs[1])
        lse_pair = jnp.where(head_a, lses[0], lses[1])
        for r in range(dil):
            o_sc[pat, pl.ds(r, rows, stride=dil), :] = o_pair[r * rows:(r + 1) * rows]
            lse_sc[pat, pl.ds(r, rows, stride=dil), :] = lse_pair[r * rows:(r + 1) * rows]

    lse = lse_sc[...]
    w = jnp.exp(lse - jnp.max(lse, axis=0, keepdims=True))
    o_ref[0] = (jnp.sum(w * o_sc[...], axis=0) / jnp.sum(w, axis=0)).astype(BF16)


def _dilated(qkvb):
    b, s, _ = qkvb.shape
    tile_spec = lambda col0, prev: pl.BlockSpec(
        (1, DIL_TILE, PAIR),
        (lambda bi, p, t: (bi, jnp.maximum(t - 1, 0), col0 + p)) if prev else (lambda bi, p, t: (bi, t, col0 + p)))
    return pl.pallas_call(
        _dilated_kernel,
        grid=(b, N_PAIRS, s // DIL_TILE),
        in_specs=[tile_spec(0, False),
                  tile_spec(N_PAIRS, False), tile_spec(N_PAIRS, True),
                  tile_spec(2 * N_PAIRS, False), tile_spec(2 * N_PAIRS, True)],
        out_specs=pl.BlockSpec((1, DIL_TILE, PAIR), lambda bi, p, t: (bi, t, p)),
        out_shape=jax.ShapeDtypeStruct((b, s, D_ATT), BF16),
        scratch_shapes=[pltpu.VMEM((len(DIL_PATTERNS), DIL_TILE, PAIR), F32),
                        pltpu.VMEM((len(DIL_PATTERNS), DIL_TILE, PAIR), F32)],
        compiler_params=pltpu.CompilerParams(
            dimension_semantics=("arbitrary", "arbitrary", "arbitrary"), vmem_limit_bytes=VMEM_LIMIT),
        name="dilated",
    )(qkvb, qkvb, qkvb, qkvb, qkvb)


def _merge_kernel(x_ref, xh_ref, oa_ref, ob_ref, gpre_ref, gpost_ref, wu_ref, wg_ref, wa_ref, wb_ref, wc_ref,
                  wout_ref, wpool_ref, pscale_ref, o_ref):
    t = pl.program_id(1)
    x = x_ref[0]
    h = _rms(x, gpre_ref[...]).astype(BF16)
    hh = _rms(xh_ref[0], gpre_ref[...]).astype(BF16)
    u = _dot(h, wu_ref[...])
    uh = jnp.where(t == 0, 0.0, _dot(hh, wu_ref[...]))
    u_ext = jnp.concatenate([uh, u], axis=0)
    pos = (t * TOK_TILE + lax.broadcasted_iota(jnp.int32, (TOK_TILE, POOL_GROUP_DIM), 0) + 1).astype(F32)
    oc = []
    for g, w in enumerate(POOL_WINDOWS):
        cols = slice(g * POOL_GROUP_DIM, (g + 1) * POOL_GROUP_DIM)
        acc = u_ext[:, cols]
        span = 1
        while span < w:
            acc = acc + pltpu.roll(acc, span, 0)
            span *= 2
        pooled = acc[POOL_HALO:] / jnp.minimum(pos, float(w)) - u[:, cols]
        oc.append(_dot(pooled.astype(BF16), wpool_ref[g]))
    oc = (jnp.concatenate(oc, axis=1) * pscale_ref[...]).astype(BF16)

    merged = None
    for br, (src, w_ref) in enumerate(((oa_ref[0], wa_ref), (ob_ref[0], wb_ref), (oc, wc_ref))):
        gz = _dot(h, wg_ref[:, br * D_MODEL:(br + 1) * D_MODEL])
        term = (1.0 / (1.0 + jnp.exp(-gz))) * _dot(src, w_ref[...])
        merged = term if merged is None else merged + term
    y = _dot(merged.astype(BF16), wout_ref[...])
    o_ref[0] = x + _rms(y, gpost_ref[...])


def _merge(x, oa, ob, gpre, gpost, wu, wg, wa, wb, wc, wout, wpool, pscale):
    b, s, d = x.shape
    const = lambda shape: pl.BlockSpec(shape, lambda bi, ti: (0,) * len(shape))
    halo_blocks = TOK_TILE // POOL_HALO
    return pl.pallas_call(
        _merge_kernel,
        grid=(b, s // TOK_TILE),
        in_specs=[
            pl.BlockSpec((1, TOK_TILE, d), lambda bi, ti: (bi, ti, 0)),
            pl.BlockSpec((1, POOL_HALO, d), lambda bi, ti: (bi, jnp.maximum(ti * halo_blocks - 1, 0), 0)),
            pl.BlockSpec((1, TOK_TILE, D_ATT), lambda bi, ti: (bi, ti, 0)),
            pl.BlockSpec((1, TOK_TILE, D_ATT), lambda bi, ti: (bi, ti, 0)),
            const((1, d)), const((1, d)),
            const((d, POOL_WIDTH)), const((d, 3 * d)),
            const((D_ATT, d)), const((D_ATT, d)), const((POOL_WIDTH, d)),
            const((d, d)), const((len(POOL_WINDOWS), POOL_GROUP_DIM, POOL_GROUP_DIM)), const((1, POOL_WIDTH)),
        ],
        out_specs=pl.BlockSpec((1, TOK_TILE, d), lambda bi, ti: (bi, ti, 0)),
        out_shape=jax.ShapeDtypeStruct((b, s, d), F32),
        compiler_params=pltpu.CompilerParams(
            dimension_semantics=("arbitrary", "arbitrary"), vmem_limit_bytes=VMEM_LIMIT),
        name="merge",
    )(x, x, oa, ob, gpre, gpost, wu, wg, wa, wb, wc, wout, wpool, pscale)


def _mlp_kernel(x_ref, gpre_ref, gpost_ref, w1_ref, w2_ref, o_ref):
    x = x_ref[0]
    h = _rms(x, gpre_ref[...]).astype(BF16)
    y = None
    for c in range(D_FF // D_MODEL):
        cols = slice(c * D_MODEL, (c + 1) * D_MODEL)
        a = jnp.maximum(_dot(h, w1_ref[:, cols]), 0.0)
        part = _dot((a * a).astype(BF16), w2_ref[cols, :])
        y = part if y is None else y + part
    o_ref[0] = x + _rms(y, gpost_ref[...])


def _mlp(x, gpre, gpost, w1, w2):
    b, s, d = x.shape
    const = lambda shape: pl.BlockSpec(shape, lambda bi, ti: (0,) * len(shape))
    return pl.pallas_call(
        _mlp_kernel,
        grid=(b, s // TOK_TILE),
        in_specs=[pl.BlockSpec((1, TOK_TILE, d), lambda bi, ti: (bi, ti, 0)),
                  const((1, d)), const((1, d)), const((d, D_FF)), const((D_FF, d))],
        out_specs=pl.BlockSpec((1, TOK_TILE, d), lambda bi, ti: (bi, ti, 0)),
        out_shape=jax.ShapeDtypeStruct((b, s, d), F32),
        compiler_params=pltpu.CompilerParams(
            dimension_semantics=("arbitrary", "arbitrary"), vmem_limit_bytes=VMEM_LIMIT),
        name="mlp",
    )(x, gpre, gpost, w1, w2)


def _layer(x, w_in, w_br_a, w_br_b, w_br_c, w_out, w_pool, pool_scale,
           g_pre_mix, g_post_mix, g_pre_mlp, g_post_mlp, w_ff1, w_ff2):
    col = lambda k: slice(k * D_ATT, (k + 1) * D_ATT)
    wtok = w_in[:, D_ATT:6 * D_ATT]
    wtok = jnp.concatenate([wtok[:, :D_ATT], wtok[:, 2 * D_ATT:]], axis=1).astype(BF16)
    wft = jnp.concatenate([w_in[:, col(0)], w_in[:, col(2)]], axis=1).T.astype(BF16)
    wu = w_in[:, 6 * D_ATT:6 * D_ATT + POOL_WIDTH].astype(BF16)
    wg = w_in[:, 6 * D_ATT + POOL_WIDTH:].astype(BF16)
    row = lambda v: v.reshape(1, -1)

    ka, qkvb, qt, bias, vt = _in_proj(x, row(g_pre_mix), wtok, wft)
    oa = _moba(qt, bias, ka, vt)
    ob = _dilated(qkvb)
    x = _merge(x, oa, ob, row(g_pre_mix), row(g_post_mix), wu, wg,
               w_br_a.astype(BF16), w_br_b.astype(BF16), w_br_c.astype(BF16), w_out.astype(BF16),
               w_pool.astype(BF16), row(pool_scale))
    return _mlp(x, row(g_pre_mlp), row(g_post_mlp), w_ff1.astype(BF16), w_ff2.astype(BF16))


@jax.jit
def kernel(x, w_in, w_br_a, w_br_b, w_br_c, w_out, w_pool, pool_scale, g_pre_mix, g_post_mix, g_pre_mlp, g_post_mlp, w_ff1, w_ff2):
    for l in range(w_in.shape[0]):
        x = _layer(x, w_in[l], w_br_a[l], w_br_b[l], w_br_c[l], w_out[l], w_pool[l], pool_scale[l],
                   g_pre_mix[l], g_post_mix[l], g_pre_mlp[l], g_post_mlp[l], w_ff1[l], w_ff2[l])
    return x
```

```python
import jax
import jax.numpy as jnp
import numpy as np
from jax import lax
from jax.experimental import pallas as pl
from jax.experimental.pallas import tpu as pltpu

F32 = jnp.float32
BF16 = jnp.bfloat16

D_MODEL = 1024
HEAD_DIM = 64
N_HEADS = 8
D_ATT = N_HEADS * HEAD_DIM
PAIR = 2 * HEAD_DIM
N_PAIRS = N_HEADS // 2
MOBA_BLOCK = 256
MOBA_TOPK = 3
MOBA_GROUP = 4
MOBA_QTILE = 512
DENOM_ROWS = 16
VT_ROWS = HEAD_DIM + DENOM_ROWS
LOG2_E = 1.4426950408889634
DIL_PATTERNS = ((128, 1), (512, 4), (2048, 16))
DIL_BAND = 128
DIL_TILE = 2048
POOL_WINDOWS = (2, 4, 8, 16)
POOL_WIDTH = 512
POOL_GROUP_DIM = 128
POOL_HALO = 16
D_FF = 4 * D_MODEL
RMS_EPS = 1e-6
QK_SCALE = HEAD_DIM ** -0.5
NEG = -1e30

TOK_TILE = 512
VMEM_LIMIT = 56 * 1024 * 1024

NT_DIMS = (((1,), (1,)), ((), ()))


def _dot(a, b):
    return jnp.dot(a, b, preferred_element_type=F32)


def _dot_nt(a, b):
    return lax.dot_general(a, b, NT_DIMS, preferred_element_type=F32)


def _split_bf16(v):
    hi = v.astype(BF16)
    lo = (v - hi.astype(F32)).astype(BF16)
    return hi, lo


def _rms(x, g):
    var = jnp.mean(x * x, axis=-1, keepdims=True)
    return x * lax.rsqrt(var + RMS_EPS) * g


def _in_proj_kernel(x_ref, g_ref, wtok_ref, wft_ref, ka_ref, qkvb_ref, qt_ref, bias_ref, vt_ref, kbar_ref):
    t = pl.program_id(1)
    blocks_per_tile = TOK_TILE // MOBA_BLOCK

    @pl.when(t == 0)
    def _():
        kbar_ref[...] = jnp.zeros_like(kbar_ref)

    h = _rms(x_ref[0], g_ref[...]).astype(BF16)

    tok = _dot(h, wtok_ref[...])
    ka = tok[:, 0:D_ATT]
    ka_ref[0] = ka.astype(BF16)
    for blk in range(blocks_per_tile):
        kbar_ref[pl.ds(t * blocks_per_tile + blk, 1), :] = jnp.mean(
            ka[blk * MOBA_BLOCK:(blk + 1) * MOBA_BLOCK], axis=0, keepdims=True)
    qkvb_ref[0, :, 0:D_ATT] = tok[:, D_ATT:2 * D_ATT] * QK_SCALE
    qkvb_ref[0, :, D_ATT:3 * D_ATT] = tok[:, 2 * D_ATT:4 * D_ATT]

    ft = _dot_nt(wft_ref[...], h)
    qt = ft[0:D_ATT] * (QK_SCALE * LOG2_E)
    qt_ref[0] = qt.astype(BF16)
    vt = ft[D_ATT:2 * D_ATT].astype(BF16)
    ones_rows = jnp.ones((DENOM_ROWS, TOK_TILE), BF16)
    for head in range(N_HEADS):
        vt_ref[0, head * VT_ROWS:head * VT_ROWS + HEAD_DIM, :] = vt[head * HEAD_DIM:(head + 1) * HEAD_DIM]
        vt_ref[0, head * VT_ROWS + HEAD_DIM:(head + 1) * VT_ROWS, :] = ones_rows

    half = D_ATT // 2
    r = lax.broadcasted_iota(jnp.int32, (half, half), 0) // HEAD_DIM
    c = lax.broadcasted_iota(jnp.int32, (half, half), 1) // HEAD_DIM
    same_head = r == c
    n_blocks = kbar_ref.shape[0]
    blk_row = lax.broadcasted_iota(jnp.int32, (n_blocks, TOK_TILE), 0).astype(F32)
    q_blk = (t * blocks_per_tile
             + lax.broadcasted_iota(jnp.int32, (n_blocks, TOK_TILE), 1) // MOBA_BLOCK).astype(F32)
    valid = blk_row < q_blk
    for quad in range(2):
        kq = kbar_ref[:, quad * half:(quad + 1) * half]
        kb = jnp.where(same_head, jnp.concatenate([kq] * 4, axis=0), 0.0)
        kb_hi, kb_lo = _split_bf16(kb)
        q_hi, q_lo = _split_bf16(qt[quad * half:(quad + 1) * half])
        gs = _dot(kb_hi, q_hi) + _dot(kb_hi, q_lo) + _dot(kb_lo, q_hi)
        for hq in range(4):
            g = jnp.where(valid, gs[hq * n_blocks:(hq + 1) * n_blocks], -jnp.inf)
            sel = jnp.zeros(g.shape, jnp.bool_)
            for _ in range(MOBA_TOPK):
                m = jnp.max(g, axis=0, keepdims=True)
                idx = jnp.min(jnp.where(g == m, blk_row, float(n_blocks)), axis=0, keepdims=True)
                pick = blk_row == idx
                sel = jnp.logical_or(sel, pick)
                g = jnp.where(pick, -jnp.inf, g)
            head = quad * 4 + hq
            bias_ref[0, head * n_blocks:(head + 1) * n_blocks, :] = jnp.where(
                jnp.logical_and(sel, valid), 0.0, NEG)


def _in_proj(x, g, wtok, wft):
    b, s, d = x.shape
    n_blocks = s // MOBA_BLOCK
    const = lambda shape: pl.BlockSpec(shape, lambda bi, ti: (0,) * len(shape))
    return pl.pallas_call(
        _in_proj_kernel,
        grid=(b, s // TOK_TILE),
        in_specs=[
            pl.BlockSpec((1, TOK_TILE, d), lambda bi, ti: (bi, ti, 0)),
            const((1, d)),
            const((d, 4 * D_ATT)), const((2 * D_ATT, d)),
        ],
        out_specs=[
            pl.BlockSpec((1, TOK_TILE, D_ATT), lambda bi, ti: (bi, ti, 0)),
            pl.BlockSpec((1, TOK_TILE, 3 * D_ATT), lambda bi, ti: (bi, ti, 0)),
            pl.BlockSpec((1, D_ATT, TOK_TILE), lambda bi, ti: (bi, 0, ti)),
            pl.BlockSpec((1, N_HEADS * n_blocks, TOK_TILE), lambda bi, ti: (bi, 0, ti)),
            pl.BlockSpec((1, N_HEADS * VT_ROWS, TOK_TILE), lambda bi, ti: (bi, 0, ti)),
        ],
        out_shape=[
            jax.ShapeDtypeStruct((b, s, D_ATT), BF16),
            jax.ShapeDtypeStruct((b, s, 3 * D_ATT), F32),
            jax.ShapeDtypeStruct((b, D_ATT, s), BF16),
            jax.ShapeDtypeStruct((b, N_HEADS * n_blocks, s), F32),
            jax.ShapeDtypeStruct((b, N_HEADS * VT_ROWS, s), BF16),
        ],
        scratch_shapes=[pltpu.VMEM((n_blocks, D_ATT), F32)],
        compiler_params=pltpu.CompilerParams(
            dimension_semantics=("arbitrary", "arbitrary"), vmem_limit_bytes=VMEM_LIMIT),
        name="in_proj",
    )(x, g, wtok, wft)


def _moba_steps(n_blocks):
    q_blocks = MOBA_QTILE // MOBA_BLOCK
    steps = [(0, 0, 1, 0)]
    for tile in range(1, n_blocks // q_blocks):
        n_past = tile * q_blocks - q_blocks
        steps.append((tile, n_past, 1, 0))
        for start in range(0, n_past, MOBA_GROUP):
            steps.append((tile, start, 0, 1 if start + MOBA_GROUP > n_past else 0))
    if len(steps) % 2 == 0:
        steps.append((steps[-1][0], steps[-1][1], 0, 2))
    return [np.asarray(col, np.int32) for col in zip(*steps)]


def _moba_kernel(tile_tab, start_tab, first_tab, off_tab, qt_ref, bias_ref, k_ref, vt_ref, o_ref, s_sc, p_sc):
    n_blocks = bias_ref.shape[1] // 2
    n_steps = tile_tab.shape[0]
    q_blocks = MOBA_QTILE // MOBA_BLOCK
    chunk_keys = MOBA_GROUP * MOBA_BLOCK
    future = (lax.broadcasted_iota(jnp.int32, (MOBA_BLOCK, MOBA_BLOCK), 0)
              > lax.broadcasted_iota(jnp.int32, (MOBA_BLOCK, MOBA_BLOCK), 1))
    qry_blk_row = lax.broadcasted_iota(jnp.int32, (1, MOBA_QTILE), 1) // MOBA_BLOCK
    zeros = jnp.zeros((HEAD_DIM, MOBA_QTILE), BF16)
    tail = tuple(range(MOBA_GROUP - q_blocks, MOBA_GROUP))

    def tile_lanes(n):
        return pl.ds(pl.multiple_of(tile_tab[n] * MOBA_QTILE, MOBA_QTILE), MOBA_QTILE)

    def chunk_keys_of(n):
        return pl.ds(pl.multiple_of(start_tab[n] * MOBA_BLOCK, q_blocks * MOBA_BLOCK), chunk_keys)

    def scores(n, slot, own_pos=tail):
        qt = qt_ref[0, :, tile_lanes(n)]
        qt_heads = (jnp.concatenate([qt[:HEAD_DIM], zeros], axis=0),
                    jnp.concatenate([zeros, qt[HEAD_DIM:]], axis=0))
        k_c = k_ref[0, chunk_keys_of(n), :]
        first = first_tab[n] == 1
        col_max = []
        for hd in range(2):
            s = _dot(k_c, qt_heads[hd])
            for r in range(MOBA_GROUP):
                s_r = s[r * MOBA_BLOCK:(r + 1) * MOBA_BLOCK]
                if r in own_pos:
                    cols = slice(own_pos.index(r) * MOBA_BLOCK, (own_pos.index(r) + 1) * MOBA_BLOCK)
                    parts = [s_r[:, :cols.start], s_r[:, cols] + jnp.where(jnp.logical_and(first, future), NEG, 0.0),
                             s_r[:, cols.stop:]]
                    s_r = jnp.concatenate([part for part in parts if part.shape[1]], axis=1)
                s_sc[slot, hd, r * MOBA_BLOCK:(r + 1) * MOBA_BLOCK, :] = s_r
                col_max.append(jnp.max(s_r, axis=0, keepdims=True))
        return tuple(col_max)

    def softmax(n, slot, m_pair, col_max, own_pos=tail):
        first = first_tab[n] == 1
        off = off_tab[n]
        alphas, m_news = [], []
        for hd in range(2):
            m_old = jnp.where(first, NEG, m_pair[hd])
            gates = []
            for r in range(MOBA_GROUP):
                gate = bias_ref[0, pl.ds(hd * n_blocks + start_tab[n] + r, 1), tile_lanes(n)]
                if r in own_pos:
                    gate = jnp.where(jnp.logical_and(first, qry_blk_row == own_pos.index(r)), 0.0, gate)
                gated_off = off >= 1 if r >= MOBA_GROUP - q_blocks else off >= 2
                gates.append(gate + jnp.where(gated_off, NEG, 0.0))
            m_new = m_old
            for r, gate in enumerate(gates):
                m_new = jnp.maximum(m_new, col_max[hd * MOBA_GROUP + r] + gate)
            alphas.append(jnp.exp2(m_old - m_new))
            m_news.append(m_new)
            for r, gate in enumerate(gates):
                rows = slice(r * MOBA_BLOCK, (r + 1) * MOBA_BLOCK)
                p_sc[slot, hd, rows, :] = jnp.exp2(s_sc[slot, hd, rows, :] - (m_new - gate)).astype(BF16)
        return tuple(alphas), tuple(m_news)

    def store(n, accs):
        ot = jnp.concatenate([a[:HEAD_DIM] * (1.0 / a[HEAD_DIM:HEAD_DIM + 1]) for a in accs], axis=0)
        o_ref[0, tile_lanes(n), :] = ot.T.astype(BF16)

    def accum(n, slot, alphas, accs):
        store(jnp.maximum(n - 1, 0), accs)
        vt_c = vt_ref[0, :, chunk_keys_of(n)]
        return tuple(alphas[hd] * accs[hd] + _dot(vt_c[hd * VT_ROWS:(hd + 1) * VT_ROWS], p_sc[slot, hd])
                     for hd in range(2))

    m0 = (jnp.full((1, MOBA_QTILE), NEG, F32),) * 2
    acc0 = (jnp.ones((VT_ROWS, MOBA_QTILE), F32),) * 2
    head = tuple(range(q_blocks))
    alpha0, m1 = softmax(0, 0, m0, scores(0, 0, own_pos=head), own_pos=head)
    cm1 = scores(1, 1)

    def two_steps(t, carry):
        alpha_prev, m, acc, cm_a = carry
        n_a = 2 * t + 1
        n_b = n_a + 1
        cm_b = scores(n_b, 0)
        alpha_a, m = softmax(n_a, 1, m, cm_a)
        acc = accum(n_a - 1, 0, alpha_prev, acc)
        cm_next = scores(jnp.minimum(n_b + 1, n_steps - 1), 1)
        alpha_b, m = softmax(n_b, 0, m, cm_b)
        acc = accum(n_a, 1, alpha_a, acc)
        return alpha_b, m, acc, cm_next

    alpha_l, _, acc, _ = lax.fori_loop(0, (n_steps - 1) // 2, two_steps, (alpha0, m1, acc0, cm1))
    store(n_steps - 1, accum(n_steps - 1, 0, alpha_l, acc))


def _moba(qt, bias, ka, vt):
    b, s, _ = ka.shape
    n_blocks = s // MOBA_BLOCK
    tables = _moba_steps(n_blocks)
    whole = lambda shape, index: pl.BlockSpec(shape, index, pipeline_mode=pl.Buffered(1))
    return pl.pallas_call(
        _moba_kernel,
        grid_spec=pltpu.PrefetchScalarGridSpec(
            num_scalar_prefetch=len(tables),
            grid=(b, N_PAIRS),
            in_specs=[
                whole((1, PAIR, s), lambda bi, p, *_: (bi, p, 0)),
                whole((1, 2 * n_blocks, s), lambda bi, p, *_: (bi, p, 0)),
                whole((1, s, PAIR), lambda bi, p, *_: (bi, 0, p)),
                whole((1, 2 * VT_ROWS, s), lambda bi, p, *_: (bi, p, 0)),
            ],
            out_specs=pl.BlockSpec((1, s, PAIR), lambda bi, p, *_: (bi, 0, p)),
            scratch_shapes=[pltpu.VMEM((2, 2, MOBA_GROUP * MOBA_BLOCK, MOBA_QTILE), F32),
                            pltpu.VMEM((2, 2, MOBA_GROUP * MOBA_BLOCK, MOBA_QTILE), BF16)],
        ),
        out_shape=jax.ShapeDtypeStruct((b, s, D_ATT), BF16),
        compiler_params=pltpu.CompilerParams(
            dimension_semantics=("arbitrary", "arbitrary"), vmem_limit_bytes=VMEM_LIMIT),
        name="moba",
    )(*tables, qt, bias, ka, vt)


def _dilated_kernel(q_ref, kc_ref, kp_ref, vc_ref, vp_ref, o_ref, o_sc, lse_sc):
    first_tile = pl.program_id(2) == 0
    n_tiles = DIL_TILE // DIL_BAND
    head_a = lax.broadcasted_iota(jnp.int32, (DIL_TILE, PAIR), 1) < HEAD_DIM
    qi = lax.broadcasted_iota(jnp.int32, (DIL_BAND, 2 * DIL_BAND), 0)
    kj = lax.broadcasted_iota(jnp.int32, (DIL_BAND, 2 * DIL_BAND), 1)
    band = jnp.where(jnp.logical_and(kj >= qi, kj <= qi + DIL_BAND), 0.0, NEG)
    band_start = jnp.where(first_tile, jnp.where(kj >= DIL_BAND, band, NEG), band)

    for pat, (window, dil) in enumerate(DIL_PATTERNS):
        rows = DIL_TILE // dil
        n_q = rows // DIL_BAND
        strided = lambda ref, start, size: ref[0, pl.ds(start, size, stride=dil), :]
        q_parts, k_tiles, v_tiles, masks = [], [], [], []
        for r in range(dil):
            q_parts.append(strided(q_ref, r, rows))
            prev_start = DIL_TILE - DIL_BAND * dil + r
            k_ext = jnp.concatenate([strided(kp_ref, prev_start, DIL_BAND), strided(kc_ref, r, rows)],
                                    axis=0).astype(BF16)
            v_ext = jnp.concatenate([strided(vp_ref, prev_start, DIL_BAND), strided(vc_ref, r, rows)],
                                    axis=0).astype(BF16)
            for qb in range(n_q):
                k_tiles.append(k_ext[qb * DIL_BAND:(qb + 2) * DIL_BAND])
                v_tiles.append(v_ext[qb * DIL_BAND:(qb + 2) * DIL_BAND])
                masks.append(band_start if qb == 0 else band)
        q_all = jnp.concatenate(q_parts, axis=0)
        mask_all = jnp.concatenate(masks, axis=0)
        outs, lses = [], []
        for hd in range(2):
            qh = jnp.where(head_a if hd == 0 else jnp.logical_not(head_a), q_all, 0.0).astype(BF16)
            s = jnp.concatenate(
                [_dot_nt(qh[n * DIL_BAND:(n + 1) * DIL_BAND], k_tiles[n]) for n in range(n_tiles)],
                axis=0) + mask_all
            m = jnp.max(s, axis=-1, keepdims=True)
            p = jnp.exp(s - m)
            l = jnp.sum(p, axis=-1, keepdims=True)
            pb = p.astype(BF16)
            o = jnp.concatenate(
                [_dot(pb[n * DIL_BAND:(n + 1) * DIL_BAND], v_tiles[n]) for n in range(n_tiles)], axis=0)
            outs.append(o / l)
            lses.append(m + jnp.log(l))
        o_pair = jnp.where(head_a, outs[0], outs[1])
        lse_pair = jnp.where(head_a, lses[0], lses[1])
        for r in range(dil):
            o_sc[pat, pl.ds(r, rows, stride=dil), :] = o_pair[r * rows:(r + 1) * rows]
            lse_sc[pat, pl.ds(r, rows, stride=dil), :] = lse_pair[r * rows:(r + 1) * rows]

    lse = lse_sc[...]
    w = jnp.exp(lse - jnp.max(lse, axis=0, keepdims=True))
    o_ref[0] = (jnp.sum(w * o_sc[...], axis=0) / jnp.sum(w, axis=0)).astype(BF16)


def _dilated(qkvb):
    b, s, _ = qkvb.shape
    tile_spec = lambda col0, prev: pl.BlockSpec(
        (1, DIL_TILE, PAIR),
        (lambda bi, p, t: (bi, jnp.maximum(t - 1, 0), col0 + p)) if prev else (lambda bi, p, t: (bi, t, col0 + p)))
    return pl.pallas_call(
        _dilated_kernel,
        grid=(b, N_PAIRS, s // DIL_TILE),
        in_specs=[tile_spec(0, False),
                  tile_spec(N_PAIRS, False), tile_spec(N_PAIRS, True),
                  tile_spec(2 * N_PAIRS, False), tile_spec(2 * N_PAIRS, True)],
        out_specs=pl.BlockSpec((1, DIL_TILE, PAIR), lambda bi, p, t: (bi, t, p)),
        out_shape=jax.ShapeDtypeStruct((b, s, D_ATT), BF16),
        scratch_shapes=[pltpu.VMEM((len(DIL_PATTERNS), DIL_TILE, PAIR), F32),
                        pltpu.VMEM((len(DIL_PATTERNS), DIL_TILE, PAIR), F32)],
        compiler_params=pltpu.CompilerParams(
            dimension_semantics=("arbitrary", "arbitrary", "arbitrary"), vmem_limit_bytes=VMEM_LIMIT),
        name="dilated",
    )(qkvb, qkvb, qkvb, qkvb, qkvb)


def _merge_kernel(x_ref, xh_ref, oa_ref, ob_ref, gpre_ref, gpost_ref, wu_ref, wg_ref, wa_ref, wb_ref, wc_ref,
                  wout_ref, wpool_ref, pscale_ref, o_ref):
    t = pl.program_id(1)
    x = x_ref[0]
    h = _rms(x, gpre_ref[...]).astype(BF16)
    hh = _rms(xh_ref[0], gpre_ref[...]).astype(BF16)
    u = _dot(h, wu_ref[...])
    uh = jnp.where(t == 0, 0.0, _dot(hh, wu_ref[...]))
    u_ext = jnp.concatenate([uh, u], axis=0)
    pos = (t * TOK_TILE + lax.broadcasted_iota(jnp.int32, (TOK_TILE, POOL_GROUP_DIM), 0) + 1).astype(F32)
    oc = []
    for g, w in enumerate(POOL_WINDOWS):
        cols = slice(g * POOL_GROUP_DIM, (g + 1) * POOL_GROUP_DIM)
        acc = u_ext[:, cols]
        span = 1
        while span < w:
            acc = acc + pltpu.roll(acc, span, 0)
            span *= 2
        pooled = acc[POOL_HALO:] / jnp.minimum(pos, float(w)) - u[:, cols]
        oc.append(_dot(pooled.astype(BF16), wpool_ref[g]))
    oc = (jnp.concatenate(oc, axis=1) * pscale_ref[...]).astype(BF16)

    merged = None
    for br, (src, w_ref) in enumerate(((oa_ref[0], wa_ref), (ob_ref[0], wb_ref), (oc, wc_ref))):
        gz = _dot(h, wg_ref[:, br * D_MODEL:(br + 1) * D_MODEL])
        term = (1.0 / (1.0 + jnp.exp(-gz))) * _dot(src, w_ref[...])
        merged = term if merged is None else merged + term
    y = _dot(merged.astype(BF16), wout_ref[...])
    o_ref[0] = x + _rms(y, gpost_ref[...])


def _merge(x, oa, ob, gpre, gpost, wu, wg, wa, wb, wc, wout, wpool, pscale):
    b, s, d = x.shape
    const = lambda shape: pl.BlockSpec(shape, lambda bi, ti: (0,) * len(shape))
    halo_blocks = TOK_TILE // POOL_HALO
    return pl.pallas_call(
        _merge_kernel,
        grid=(b, s // TOK_TILE),
        in_specs=[
            pl.BlockSpec((1, TOK_TILE, d), lambda bi, ti: (bi, ti, 0)),
            pl.BlockSpec((1, POOL_HALO, d), lambda bi, ti: (bi, jnp.maximum(ti * halo_blocks - 1, 0), 0)),
            pl.BlockSpec((1, TOK_TILE, D_ATT), lambda bi, ti: (bi, ti, 0)),
            pl.BlockSpec((1, TOK_TILE, D_ATT), lambda bi, ti: (bi, ti, 0)),
            const((1, d)), const((1, d)),
            const((d, POOL_WIDTH)), const((d, 3 * d)),
            const((D_ATT, d)), const((D_ATT, d)), const((POOL_WIDTH, d)),
            const((d, d)), const((len(POOL_WINDOWS), POOL_GROUP_DIM, POOL_GROUP_DIM)), const((1, POOL_WIDTH)),
        ],
        out_specs=pl.BlockSpec((1, TOK_TILE, d), lambda bi, ti: (bi, ti, 0)),
        out_shape=jax.ShapeDtypeStruct((b, s, d), F32),
        compiler_params=pltpu.CompilerParams(
            dimension_semantics=("arbitrary", "arbitrary"), vmem_limit_bytes=VMEM_LIMIT),
        name="merge",
    )(x, x, oa, ob, gpre, gpost, wu, wg, wa, wb, wc, wout, wpool, pscale)


def _mlp_kernel(x_ref, gpre_ref, gpost_ref, w1_ref, w2_ref, o_ref):
    x = x_ref[0]
    h = _rms(x, gpre_ref[...]).astype(BF16)
    y = None
    for c in range(D_FF // D_MODEL):
        cols = slice(c * D_MODEL, (c + 1) * D_MODEL)
        a = jnp.maximum(_dot(h, w1_ref[:, cols]), 0.0)
        part = _dot((a * a).astype(BF16), w2_ref[cols, :])
        y = part if y is None else y + part
    o_ref[0] = x + _rms(y, gpost_ref[...])


def _mlp(x, gpre, gpost, w1, w2):
    b, s, d = x.shape
    const = lambda shape: pl.BlockSpec(shape, lambda bi, ti: (0,) * len(shape))
    return pl.pallas_call(
        _mlp_kernel,
        grid=(b, s // TOK_TILE),
        in_specs=[pl.BlockSpec((1, TOK_TILE, d), lambda bi, ti: (bi, ti, 0)),
                  const((1, d)), const((1, d)), const((d, D_FF)), const((D_FF, d))],
        out_specs=pl.BlockSpec((1, TOK_TILE, d), lambda bi, ti: (bi, ti, 0)),
        out_shape=jax.ShapeDtypeStruct((b, s, d), F32),
        compiler_params=pltpu.CompilerParams(
            dimension_semantics=("arbitrary", "arbitrary"), vmem_limit_bytes=VMEM_LIMIT),
        name="mlp",
    )(x, gpre, gpost, w1, w2)


def _layer(x, w_in, w_br_a, w_br_b, w_br_c, w_out, w_pool, pool_scale,
           g_pre_mix, g_post_mix, g_pre_mlp, g_post_mlp, w_ff1, w_ff2):
    col = lambda k: slice(k * D_ATT, (k + 1) * D_ATT)
    wtok = w_in[:, D_ATT:6 * D_ATT]
    wtok = jnp.concatenate([wtok[:, :D_ATT], wtok[:, 2 * D_ATT:]], axis=1).astype(BF16)
    wft = jnp.concatenate([w_in[:, col(0)], w_in[:, col(2)]], axis=1).T.astype(BF16)
    wu = w_in[:, 6 * D_ATT:6 * D_ATT + POOL_WIDTH].astype(BF16)
    wg = w_in[:, 6 * D_ATT + POOL_WIDTH:].astype(BF16)
    row = lambda v: v.reshape(1, -1)

    ka, qkvb, qt, bias, vt = _in_proj(x, row(g_pre_mix), wtok, wft)
    oa = _moba(qt, bias, ka, vt)
    ob = _dilated(qkvb)
    x = _merge(x, oa, ob, row(g_pre_mix), row(g_post_mix), wu, wg,
               w_br_a.astype(BF16), w_br_b.astype(BF16), w_br_c.astype(BF16), w_out.astype(BF16),
               w_pool.astype(BF16), row(pool_scale))
    return _mlp(x, row(g_pre_mlp), row(g_post_mlp), w_ff1.astype(BF16), w_ff2.astype(BF16))


@jax.jit
def kernel(x, w_in, w_br_a, w_br_b, w_br_c, w_out, w_pool, pool_scale, g_pre_mix, g_post_mix, g_pre_mlp, g_post_mlp, w_ff1, w_ff2):
    for l in range(w_in.shape[0]):
        x = _layer(x, w_in[l], w_br_a[l], w_br_b[l], w_br_c[l], w_out[l], w_pool[l], pool_scale[l],
                   g_pre_mix[l], g_post_mix[l], g_pre_mlp[l], g_post_mlp[l], w_ff1[l], w_ff2[l])
    return x
```

```python
import jax
import jax.numpy as jnp
import numpy as np
from jax import lax
from jax.experimental import pallas as pl
from jax.experimental.pallas import tpu as pltpu

F32 = jnp.float32
BF16 = jnp.bfloat16

D_MODEL = 1024
HEAD_DIM = 64
N_HEADS = 8
D_ATT = N_HEADS * HEAD_DIM
PAIR = 2 * HEAD_DIM
N_PAIRS = N_HEADS // 2
MOBA_BLOCK = 256
MOBA_TOPK = 3
MOBA_GROUP = 4
MOBA_QTILE = 512
DENOM_ROWS = 16
VT_ROWS = HEAD_DIM + DENOM_ROWS
LOG2_E = 1.4426950408889634
DIL_PATTERNS = ((128, 1), (512, 4), (2048, 16))
DIL_BAND = 128
DIL_TILE = 2048
POOL_WINDOWS = (2, 4, 8, 16)
POOL_WIDTH = 512
POOL_GROUP_DIM = 128
POOL_HALO = 16
D_FF = 4 * D_MODEL
RMS_EPS = 1e-6
QK_SCALE = HEAD_DIM ** -0.5
NEG = -1e30

TOK_TILE = 512
VMEM_LIMIT = 56 * 1024 * 1024

NT_DIMS = (((1,), (1,)), ((), ()))


def _dot(a, b):
    return jnp.dot(a, b, preferred_element_type=F32)


def _dot_nt(a, b):
    return lax.dot_general(a, b, NT_DIMS, preferred_element_type=F32)


def _split_bf16(v):
    hi = v.astype(BF16)
    lo = (v - hi.astype(F32)).astype(BF16)
    return hi, lo


def _rms(x, g):
    var = jnp.mean(x * x, axis=-1, keepdims=True)
    return x * lax.rsqrt(var + RMS_EPS) * g


def _in_proj_kernel(x_ref, g_ref, wtok_ref, wft_ref, ka_ref, qkvb_ref, qt_ref, bias_ref, vt_ref, kbar_ref):
    t = pl.program_id(1)
    blocks_per_tile = TOK_TILE // MOBA_BLOCK

    @pl.when(t == 0)
    def _():
        kbar_ref[...] = jnp.zeros_like(kbar_ref)

    h = _rms(x_ref[0], g_ref[...]).astype(BF16)

    tok = _dot(h, wtok_ref[...])
    ka = tok[:, 0:D_ATT]
    ka_ref[0] = ka.astype(BF16)
    for blk in range(blocks_per_tile):
        kbar_ref[pl.ds(t * blocks_per_tile + blk, 1), :] = jnp.mean(
            ka[blk * MOBA_BLOCK:(blk + 1) * MOBA_BLOCK], axis=0, keepdims=True)
    qkvb_ref[0, :, 0:D_ATT] = tok[:, D_ATT:2 * D_ATT] * QK_SCALE
    qkvb_ref[0, :, D_ATT:3 * D_ATT] = tok[:, 2 * D_ATT:4 * D_ATT]

    ft = _dot_nt(wft_ref[...], h)
    qt = ft[0:D_ATT] * (QK_SCALE * LOG2_E)
    qt_ref[0] = qt.astype(BF16)
    vt = ft[D_ATT:2 * D_ATT].astype(BF16)
    ones_rows = jnp.ones((DENOM_ROWS, TOK_TILE), BF16)
    for head in range(N_HEADS):
        vt_ref[0, head * VT_ROWS:head * VT_ROWS + HEAD_DIM, :] = vt[head * HEAD_DIM:(head + 1) * HEAD_DIM]
        vt_ref[0, head * VT_ROWS + HEAD_DIM:(head + 1) * VT_ROWS, :] = ones_rows

    half = D_ATT // 2
    r = lax.broadcasted_iota(jnp.int32, (half, half), 0) // HEAD_DIM
    c = lax.broadcasted_iota(jnp.int32, (half, half), 1) // HEAD_DIM
    same_head = r == c
    n_blocks = kbar_ref.shape[0]
    blk_row = lax.broadcasted_iota(jnp.int32, (n_blocks, TOK_TILE), 0).astype(F32)
    q_blk = (t * blocks_per_tile
             + lax.broadcasted_iota(jnp.int32, (n_blocks, TOK_TILE), 1) // MOBA_BLOCK).astype(F32)
    valid = blk_row < q_blk
    for quad in range(2):
        kq = kbar_ref[:, quad * half:(quad + 1) * half]
        kb = jnp.where(same_head, jnp.concatenate([kq] * 4, axis=0), 0.0)
        kb_hi, kb_lo = _split_bf16(kb)
        q_hi, q_lo = _split_bf16(qt[quad * half:(quad + 1) * half])
        gs = _dot(kb_hi, q_hi) + _dot(kb_hi, q_lo) + _dot(kb_lo, q_hi)
        for hq in range(4):
            g = jnp.where(valid, gs[hq * n_blocks:(hq + 1) * n_blocks], -jnp.inf)
            sel = jnp.zeros(g.shape, jnp.bool_)
            for _ in range(MOBA_TOPK):
                m = jnp.max(g, axis=0, keepdims=True)
                idx = jnp.min(jnp.where(g == m, blk_row, float(n_blocks)), axis=0, keepdims=True)
                pick = blk_row == idx
                sel = jnp.logical_or(sel, pick)
                g = jnp.where(pick, -jnp.inf, g)
            head = quad * 4 + hq
            bias_ref[0, head * n_blocks:(head + 1) * n_blocks, :] = jnp.where(
                jnp.logical_and(sel, valid), 0.0, NEG)


def _in_proj(x, g, wtok, wft):
    b, s, d = x.shape
    n_blocks = s // MOBA_BLOCK
    const = lambda shape: pl.BlockSpec(shape, lambda bi, ti: (0,) * len(shape))
    return pl.pallas_call(
        _in_proj_kernel,
        grid=(b, s // TOK_TILE),
        in_specs=[
            pl.BlockSpec((1, TOK_TILE, d), lambda bi, ti: (bi, ti, 0)),
            const((1, d)),
            const((d, 4 * D_ATT)), const((2 * D_ATT, d)),
        ],
        out_specs=[
            pl.BlockSpec((1, TOK_TILE, D_ATT), lambda bi, ti: (bi, ti, 0)),
            pl.BlockSpec((1, TOK_TILE, 3 * D_ATT), lambda bi, ti: (bi, ti, 0)),
            pl.BlockSpec((1, D_ATT, TOK_TILE), lambda bi, ti: (bi, 0, ti)),
            pl.BlockSpec((1, N_HEADS * n_blocks, TOK_TILE), lambda bi, ti: (bi, 0, ti)),
            pl.BlockSpec((1, N_HEADS * VT_ROWS, TOK_TILE), lambda bi, ti: (bi, 0, ti)),
        ],
        out_shape=[
            jax.ShapeDtypeStruct((b, s, D_ATT), BF16),
            jax.ShapeDtypeStruct((b, s, 3 * D_ATT), F32),
            jax.ShapeDtypeStruct((b, D_ATT, s), BF16),
            jax.ShapeDtypeStruct((b, N_HEADS * n_blocks, s), F32),
            jax.ShapeDtypeStruct((b, N_HEADS * VT_ROWS, s), BF16),
        ],
        scratch_shapes=[pltpu.VMEM((n_blocks, D_ATT), F32)],
        compiler_params=pltpu.CompilerParams(
            dimension_semantics=("arbitrary", "arbitrary"), vmem_limit_bytes=VMEM_LIMIT),
        name="in_proj",
    )(x, g, wtok, wft)


def _moba_steps(n_blocks):
    q_blocks = MOBA_QTILE // MOBA_BLOCK
    steps = [(0, 0, 1, 0)]
    for tile in range(1, n_blocks // q_blocks):
        n_past = tile * q_blocks - q_blocks
        steps.append((tile, n_past, 1, 0))
        for start in range(0, n_past, MOBA_GROUP):
            steps.append((tile, start, 0, 1 if start + MOBA_GROUP > n_past else 0))
    if len(steps) % 2 == 0:
        steps.append((steps[-1][0], steps[-1][1], 0, 2))
    return [np.asarray(col, np.int32) for col in zip(*steps)]


def _moba_kernel(tile_tab, start_tab, first_tab, off_tab, qt_ref, bias_ref, k_ref, vt_ref, o_ref, s_sc, p_sc):
    n_blocks = bias_ref.shape[1] // 2
    n_steps = tile_tab.shape[0]
    q_blocks = MOBA_QTILE // MOBA_BLOCK
    chunk_keys = MOBA_GROUP * MOBA_BLOCK
    future = (lax.broadcasted_iota(jnp.int32, (MOBA_BLOCK, MOBA_BLOCK), 0)
              > lax.broadcasted_iota(jnp.int32, (MOBA_BLOCK, MOBA_BLOCK), 1))
    qry_blk_row = lax.broadcasted_iota(jnp.int32, (1, MOBA_QTILE), 1) // MOBA_BLOCK
    zeros = jnp.zeros((HEAD_DIM, MOBA_QTILE), BF16)
    tail = tuple(range(MOBA_GROUP - q_blocks, MOBA_GROUP))

    def tile_lanes(n):
        return pl.ds(pl.multiple_of(tile_tab[n] * MOBA_QTILE, MOBA_QTILE), MOBA_QTILE)

    def chunk_keys_of(n):
        return pl.ds(pl.multiple_of(start_tab[n] * MOBA_BLOCK, q_blocks * MOBA_BLOCK), chunk_keys)

    def scores(n, slot, own_pos=tail):
        qt = qt_ref[0, :, tile_lanes(n)]
        qt_heads = (jnp.concatenate([qt[:HEAD_DIM], zeros], axis=0),
                    jnp.concatenate([zeros, qt[HEAD_DIM:]], axis=0))
        k_c = k_ref[0, chunk_keys_of(n), :]
        first = first_tab[n] == 1
        col_max = []
        for hd in range(2):
            s = _dot(k_c, qt_heads[hd])
            for r in range(MOBA_GROUP):
                s_r = s[r * MOBA_BLOCK:(r + 1) * MOBA_BLOCK]
                if r in own_pos:
                    cols = slice(own_pos.index(r) * MOBA_BLOCK, (own_pos.index(r) + 1) * MOBA_BLOCK)
                    parts = [s_r[:, :cols.start], s_r[:, cols] + jnp.where(jnp.logical_and(first, future), NEG, 0.0),
                             s_r[:, cols.stop:]]
                    s_r = jnp.concatenate([part for part in parts if part.shape[1]], axis=1)
                s_sc[slot, hd, r * MOBA_BLOCK:(r + 1) * MOBA_BLOCK, :] = s_r
                col_max.append(jnp.max(s_r, axis=0, keepdims=True))
        return tuple(col_max)

    def softmax(n, slot, m_pair, col_max, own_pos=tail):
        first = first_tab[n] == 1
        off = off_tab[n]
        alphas, m_news = [], []
        for hd in range(2):
            m_old = jnp.where(first, NEG, m_pair[hd])
            gates = []
            for r in range(MOBA_GROUP):
                gate = bias_ref[0, pl.ds(hd * n_blocks + start_tab[n] + r, 1), tile_lanes(n)]
                if r in own_pos:
                    gate = jnp.where(jnp.logical_and(first, qry_blk_row == own_pos.index(r)), 0.0, gate)
                gated_off = off >= 1 if r >= MOBA_GROUP - q_blocks else off >= 2
                gates.append(gate + jnp.where(gated_off, NEG, 0.0))
            m_new = m_old
            for r, gate in enumerate(gates):
                m_new = jnp.maximum(m_new, col_max[hd * MOBA_GROUP + r] + gate)
            alphas.append(jnp.exp2(m_old - m_new))
            m_news.append(m_new)
            for r, gate in enumerate(gates):
                rows = slice(r * MOBA_BLOCK, (r + 1) * MOBA_BLOCK)
                p_sc[slot, hd, rows, :] = jnp.exp2(s_sc[slot, hd, rows, :] - (m_new - gate)).astype(BF16)
        return tuple(alphas), tuple(m_news)

    def store(n, accs):
        ot = jnp.concatenate([a[:HEAD_DIM] * (1.0 / a[HEAD_DIM:HEAD_DIM + 1]) for a in accs], axis=0)
        o_ref[0, tile_lanes(n), :] = ot.T.astype(BF16)

    def accum(n, slot, alphas, accs):
        store(jnp.maximum(n - 1, 0), accs)
        vt_c = vt_ref[0, :, chunk_keys_of(n)]
        return tuple(alphas[hd] * accs[hd] + _dot(vt_c[hd * VT_ROWS:(hd + 1) * VT_ROWS], p_sc[slot, hd])
                     for hd in range(2))

    m0 = (jnp.full((1, MOBA_QTILE), NEG, F32),) * 2
    acc0 = (jnp.ones((VT_ROWS, MOBA_QTILE), F32),) * 2
    head = tuple(range(q_blocks))
    alpha0, m1 = softmax(0, 0, m0, scores(0, 0, own_pos=head), own_pos=head)
    cm1 = scores(1, 1)

    def two_steps(t, carry):
        alpha_prev, m, acc, cm_a = carry
        n_a = 2 * t + 1
        n_b = n_a + 1
        cm_b = scores(n_b, 0)
        alpha_a, m = softmax(n_a, 1, m, cm_a)
        acc = accum(n_a - 1, 0, alpha_prev, acc)
        cm_next = scores(jnp.minimum(n_b + 1, n_steps - 1), 1)
        alpha_b, m = softmax(n_b, 0, m, cm_b)
        acc = accum(n_a, 1, alpha_a, acc)
        return alpha_b, m, acc, cm_next

    alpha_l, _, acc, _ = lax.fori_loop(0, (n_steps - 1) // 2, two_steps, (alpha0, m1, acc0, cm1))
    store(n_steps - 1, accum(n_steps - 1, 0, alpha_l, acc))


def _moba(qt, bias, ka, vt):
    b, s, _ = ka.shape
    n_blocks = s // MOBA_BLOCK
    tables = _moba_steps(n_blocks)
    single = pl.Buffered(1)
    return pl.pallas_call(
        _moba_kernel,
        grid_spec=pltpu.PrefetchScalarGridSpec(
            num_scalar_prefetch=len(tables),
            grid=(b, N_PAIRS),
            in_specs=[
                pl.BlockSpec((1, PAIR, s), lambda bi, p, *_: (bi, p, 0)),
                pl.BlockSpec((1, 2 * n_blocks, s), lambda bi, p, *_: (bi, p, 0), pipeline_mode=single),
                pl.BlockSpec((1, s, PAIR), lambda bi, p, *_: (bi, 0, p)),
                pl.BlockSpec((1, 2 * VT_ROWS, s), lambda bi, p, *_: (bi, p, 0)),
            ],
            out_specs=pl.BlockSpec((1, s, PAIR), lambda bi, p, *_: (bi, 0, p), pipeline_mode=single),
            scratch_shapes=[pltpu.VMEM((2, 2, MOBA_GROUP * MOBA_BLOCK, MOBA_QTILE), F32),
                            pltpu.VMEM((2, 2, MOBA_GROUP * MOBA_BLOCK, MOBA_QTILE), BF16)],
        ),
        out_shape=jax.ShapeDtypeStruct((b, s, D_ATT), BF16),
        compiler_params=pltpu.CompilerParams(
            dimension_semantics=("arbitrary", "arbitrary"), vmem_limit_bytes=VMEM_LIMIT),
        name="moba",
    )(*tables, qt, bias, ka, vt)


def _dilated_kernel(q_ref, kc_ref, kp_ref, vc_ref, vp_ref, o_ref, o_sc, lse_sc):
    first_tile = pl.program_id(2) == 0
    n_tiles = DIL_TILE // DIL_BAND
    head_a = lax.broadcasted_iota(jnp.int32, (DIL_TILE, PAIR), 1) < HEAD_DIM
    qi = lax.broadcasted_iota(jnp.int32, (DIL_BAND, 2 * DIL_BAND), 0)
    kj = lax.broadcasted_iota(jnp.int32, (DIL_BAND, 2 * DIL_BAND), 1)
    band = jnp.where(jnp.logical_and(kj >= qi, kj <= qi + DIL_BAND), 0.0, NEG)
    band_start = jnp.where(first_tile, jnp.where(kj >= DIL_BAND, band, NEG), band)

    for pat, (window, dil) in enumerate(DIL_PATTERNS):
        rows = DIL_TILE // dil
        n_q = rows // DIL_BAND
        strided = lambda ref, start, size: ref[0, pl.ds(start, size, stride=dil), :]
        q_parts, k_tiles, v_tiles, masks = [], [], [], []
        for r in range(dil):
            q_parts.append(strided(q_ref, r, rows))
            prev_start = DIL_TILE - DIL_BAND * dil + r
            k_ext = jnp.concatenate([strided(kp_ref, prev_start, DIL_BAND), strided(kc_ref, r, rows)],
                                    axis=0).astype(BF16)
            v_ext = jnp.concatenate([strided(vp_ref, prev_start, DIL_BAND), strided(vc_ref, r, rows)],
                                    axis=0).astype(BF16)
            for qb in range(n_q):
                k_tiles.append(k_ext[qb * DIL_BAND:(qb + 2) * DIL_BAND])
                v_tiles.append(v_ext[qb * DIL_BAND:(qb + 2) * DIL_BAND])
                masks.append(band_start if qb == 0 else band)
        q_all = jnp.concatenate(q_parts, axis=0)
        mask_all = jnp.concatenate(masks, axis=0)
        outs, lses = [], []
        for hd in range(2):
            qh = jnp.where(head_a if hd == 0 else jnp.logical_not(head_a), q_all, 0.0).astype(BF16)
            s = jnp.concatenate(
                [_dot_nt(qh[n * DIL_BAND:(n + 1) * DIL_BAND], k_tiles[n]) for n in range(n_tiles)],
                axis=0) + mask_all
            m = jnp.max(s, axis=-1, keepdims=True)
            p = jnp.exp(s - m)
            l = jnp.sum(p, axis=-1, keepdims=True)
            pb = p.astype(BF16)
            o = jnp.concatenate(
                [_dot(pb[n * DIL_BAND:(n + 1) * DIL_BAND], v_tiles[n]) for n in range(n_tiles)], axis=0)
            outs.append(o / l)
            lses.append(m + jnp.log(l))
        o_pair = jnp.where(head_a, outs[0], outs[1])
        lse_pair = jnp.where(head_a, lses[0], lses[1])
        for r in range(dil):
            o_sc[pat, pl.ds(r, rows, stride=dil), :] = o_pair[r * rows:(r + 1) * rows]
            lse_sc[pat, pl.ds(r, rows, stride=dil), :] = lse_pair[r * rows:(r + 1) * rows]

    lse = lse_sc[...]
    w = jnp.exp(lse - jnp.max(lse, axis=0, keepdims=True))
    o_ref[0] = (jnp.sum(w * o_sc[...], axis=0) / jnp.sum(w, axis=0)).astype(BF16)


def _dilated(qkvb):
    b, s, _ = qkvb.shape
    tile_spec = lambda col0, prev: pl.BlockSpec(
        (1, DIL_TILE, PAIR),
        (lambda bi, p, t: (bi, jnp.maximum(t - 1, 0), col0 + p)) if prev else (lambda bi, p, t: (bi, t, col0 + p)))
    return pl.pallas_call(
        _dilated_kernel,
        grid=(b, N_PAIRS, s // DIL_TILE),
        in_specs=[tile_spec(0, False),
                  tile_spec(N_PAIRS, False), tile_spec(N_PAIRS, True),
                  tile_spec(2 * N_PAIRS, False), tile_spec(2 * N_PAIRS, True)],
        out_specs=pl.BlockSpec((1, DIL_TILE, PAIR), lambda bi, p, t: (bi, t, p)),
        out_shape=jax.ShapeDtypeStruct((b, s, D_ATT), BF16),
        scratch_shapes=[pltpu.VMEM((len(DIL_PATTERNS), DIL_TILE, PAIR), F32),
                        pltpu.VMEM((len(DIL_PATTERNS), DIL_TILE, PAIR), F32)],
        compiler_params=pltpu.CompilerParams(
            dimension_semantics=("arbitrary", "arbitrary", "arbitrary"), vmem_limit_bytes=VMEM_LIMIT),
        name="dilated",
    )(qkvb, qkvb, qkvb, qkvb, qkvb)


def _merge_kernel(x_ref, xh_ref, oa_ref, ob_ref, gpre_ref, gpost_ref, wu_ref, wg_ref, wa_ref, wb_ref, wc_ref,
                  wout_ref, wpool_ref, pscale_ref, o_ref):
    t = pl.program_id(1)
    x = x_ref[0]
    h = _rms(x, gpre_ref[...]).astype(BF16)
    hh = _rms(xh_ref[0], gpre_ref[...]).astype(BF16)
    u = _dot(h, wu_ref[...])
    uh = jnp.where(t == 0, 0.0, _dot(hh, wu_ref[...]))
    u_ext = jnp.concatenate([uh, u], axis=0)
    pos = (t * TOK_TILE + lax.broadcasted_iota(jnp.int32, (TOK_TILE, POOL_GROUP_DIM), 0) + 1).astype(F32)
    oc = []
    for g, w in enumerate(POOL_WINDOWS):
        cols = slice(g * POOL_GROUP_DIM, (g + 1) * POOL_GROUP_DIM)
        acc = u_ext[:, cols]
        span = 1
        while span < w:
            acc = acc + pltpu.roll(acc, span, 0)
            span *= 2
        pooled = acc[POOL_HALO:] / jnp.minimum(pos, float(w)) - u[:, cols]
        oc.append(_dot(pooled.astype(BF16), wpool_ref[g]))
    oc = (jnp.concatenate(oc, axis=1) * pscale_ref[...]).astype(BF16)

    merged = None
    for br, (src, w_ref) in enumerate(((oa_ref[0], wa_ref), (ob_ref[0], wb_ref), (oc, wc_ref))):
        gz = _dot(h, wg_ref[:, br * D_MODEL:(br + 1) * D_MODEL])
        term = (1.0 / (1.0 + jnp.exp(-gz))) * _dot(src, w_ref[...])
        merged = term if merged is None else merged + term
    y = _dot(merged.astype(BF16), wout_ref[...])
    o_ref[0] = x + _rms(y, gpost_ref[...])


def _merge(x, oa, ob, gpre, gpost, wu, wg, wa, wb, wc, wout, wpool, pscale):
    b, s, d = x.shape
    const = lambda shape: pl.BlockSpec(shape, lambda bi, ti: (0,) * len(shape))
    halo_blocks = TOK_TILE // POOL_HALO
    return pl.pallas_call(
        _merge_kernel,
        grid=(b, s // TOK_TILE),
        in_specs=[
            pl.BlockSpec((1, TOK_TILE, d), lambda bi, ti: (bi, ti, 0)),
            pl.BlockSpec((1, POOL_HALO, d), lambda bi, ti: (bi, jnp.maximum(ti * halo_blocks - 1, 0), 0)),
            pl.BlockSpec((1, TOK_TILE, D_ATT), lambda bi, ti: (bi, ti, 0)),
            pl.BlockSpec((1, TOK_TILE, D_ATT), lambda bi, ti: (bi, ti, 0)),
            const((1, d)), const((1, d)),
            const((d, POOL_WIDTH)), const((d, 3 * d)),
            const((D_ATT, d)), const((D_ATT, d)), const((POOL_WIDTH, d)),
            const((d, d)), const((len(POOL_WINDOWS), POOL_GROUP_DIM, POOL_GROUP_DIM)), const((1, POOL_WIDTH)),
        ],
        out_specs=pl.BlockSpec((1, TOK_TILE, d), lambda bi, ti: (bi, ti, 0)),
        out_shape=jax.ShapeDtypeStruct((b, s, d), F32),
        compiler_params=pltpu.CompilerParams(
            dimension_semantics=("arbitrary", "arbitrary"), vmem_limit_bytes=VMEM_LIMIT),
        name="merge",
    )(x, x, oa, ob, gpre, gpost, wu, wg, wa, wb, wc, wout, wpool, pscale)


def _mlp_kernel(x_ref, gpre_ref, gpost_ref, w1_ref, w2_ref, o_ref):
    x = x_ref[0]
    h = _rms(x, gpre_ref[...]).astype(BF16)
    y = None
    for c in range(D_FF // D_MODEL):
        cols = slice(c * D_MODEL, (c + 1) * D_MODEL)
        a = jnp.maximum(_dot(h, w1_ref[:, cols]), 0.0)
        part = _dot((a * a).astype(BF16), w2_ref[cols, :])
        y = part if y is None else y + part
    o_ref[0] = x + _rms(y, gpost_ref[...])


def _mlp(x, gpre, gpost, w1, w2):
    b, s, d = x.shape
    const = lambda shape: pl.BlockSpec(shape, lambda bi, ti: (0,) * len(shape))
    return pl.pallas_call(
        _mlp_kernel,
        grid=(b, s // TOK_TILE),
        in_specs=[pl.BlockSpec((1, TOK_TILE, d), lambda bi, ti: (bi, ti, 0)),
                  const((1, d)), const((1, d)), const((d, D_FF)), const((D_FF, d))],
        out_specs=pl.BlockSpec((1, TOK_TILE, d), lambda bi, ti: (bi, ti, 0)),
        out_shape=jax.ShapeDtypeStruct((b, s, d), F32),
        compiler_params=pltpu.CompilerParams(
            dimension_semantics=("arbitrary", "arbitrary"), vmem_limit_bytes=VMEM_LIMIT),
        name="mlp",
    )(x, gpre, gpost, w1, w2)


def _layer(x, w_in, w_br_a, w_br_b, w_br_c, w_out, w_pool, pool_scale,
           g_pre_mix, g_post_mix, g_pre_mlp, g_post_mlp, w_ff1, w_ff2):
    col = lambda k: slice(k * D_ATT, (k + 1) * D_ATT)
    wtok = w_in[:, D_ATT:6 * D_ATT]
    wtok = jnp.concatenate([wtok[:, :D_ATT], wtok[:, 2 * D_ATT:]], axis=1).astype(BF16)
    wft = jnp.concatenate([w_in[:, col(0)], w_in[:, col(2)]], axis=1).T.astype(BF16)
    wu = w_in[:, 6 * D_ATT:6 * D_ATT + POOL_WIDTH].astype(BF16)
    wg = w_in[:, 6 * D_ATT + POOL_WIDTH:].astype(BF16)
    row = lambda v: v.reshape(1, -1)

    ka, qkvb, qt, bias, vt = _in_proj(x, row(g_pre_mix), wtok, wft)
    oa = _moba(qt, bias, ka, vt)
    ob = _dilated(qkvb)
    x = _merge(x, oa, ob, row(g_pre_mix), row(g_post_mix), wu, wg,
               w_br_a.astype(BF16), w_br_b.astype(BF16), w_br_c.astype(BF16), w_out.astype(BF16),
               w_pool.astype(BF16), row(pool_scale))
    return _mlp(x, row(g_pre_mlp), row(g_post_mlp), w_ff1.astype(BF16), w_ff2.astype(BF16))


@jax.jit
def kernel(x, w_in, w_br_a, w_br_b, w_br_c, w_out, w_pool, pool_scale, g_pre_mix, g_post_mix, g_pre_mlp, g_post_mlp, w_ff1, w_ff2):
    for l in range(w_in.shape[0]):
        x = _layer(x, w_in[l], w_br_a[l], w_br_b[l], w_br_c[l], w_out[l], w_pool[l], pool_scale[l],
                   g_pre_mix[l], g_post_mix[l], g_pre_mlp[l], g_post_mlp[l], w_ff1[l], w_ff2[l])
    return x
```

```python
import jax
import jax.numpy as jnp
import numpy as np
from jax import lax
from jax.experimental import pallas as pl
from jax.experimental.pallas import tpu as pltpu

F32 = jnp.float32
BF16 = jnp.bfloat16

D_MODEL = 1024
HEAD_DIM = 64
N_HEADS = 8
D_ATT = N_HEADS * HEAD_DIM
PAIR = 2 * HEAD_DIM
N_PAIRS = N_HEADS // 2
MOBA_BLOCK = 256
MOBA_TOPK = 3
MOBA_GROUP = 4
MOBA_QTILE = 512
DENOM_ROWS = 16
VT_ROWS = HEAD_DIM + DENOM_ROWS
LOG2_E = 1.4426950408889634
DIL_PATTERNS = ((128, 1), (512, 4), (2048, 16))
DIL_BAND = 128
DIL_TILE = 2048
DIL_SPLIT = 4
POOL_WINDOWS = (2, 4, 8, 16)
POOL_WIDTH = 512
POOL_GROUP_DIM = 128
POOL_HALO = 16
D_FF = 4 * D_MODEL
RMS_EPS = 1e-6
QK_SCALE = HEAD_DIM ** -0.5
NEG = -1e30

TOK_TILE = 512
VMEM_LIMIT = 56 * 1024 * 1024

NT_DIMS = (((1,), (1,)), ((), ()))


def _dot(a, b):
    return jnp.dot(a, b, preferred_element_type=F32)


def _dot_nt(a, b):
    return lax.dot_general(a, b, NT_DIMS, preferred_element_type=F32)


def _split_bf16(v):
    hi = v.astype(BF16)
    lo = (v - hi.astype(F32)).astype(BF16)
    return hi, lo


def _rms(x, g):
    var = jnp.mean(x * x, axis=-1, keepdims=True)
    return x * lax.rsqrt(var + RMS_EPS) * g


def _in_proj_kernel(x_ref, g_ref, wtok_ref, wft_ref, ka_ref, qkvb_ref, qt_ref, bias_ref, vt_ref, kbar_ref):
    t = pl.program_id(1)
    blocks_per_tile = TOK_TILE // MOBA_BLOCK

    @pl.when(t == 0)
    def _():
        kbar_ref[...] = jnp.zeros_like(kbar_ref)

    h = _rms(x_ref[0], g_ref[...]).astype(BF16)

    tok = _dot(h, wtok_ref[...])
    ka = tok[:, 0:D_ATT]
    ka_ref[0] = ka.astype(BF16)
    for blk in range(blocks_per_tile):
        kbar_ref[pl.ds(t * blocks_per_tile + blk, 1), :] = jnp.mean(
            ka[blk * MOBA_BLOCK:(blk + 1) * MOBA_BLOCK], axis=0, keepdims=True)
    qkvb_ref[0, :, 0:D_ATT] = tok[:, D_ATT:2 * D_ATT] * QK_SCALE
    qkvb_ref[0, :, D_ATT:3 * D_ATT] = tok[:, 2 * D_ATT:4 * D_ATT]

    ft = _dot_nt(wft_ref[...], h)
    qt = ft[0:D_ATT] * (QK_SCALE * LOG2_E)
    qt_ref[0] = qt.astype(BF16)
    vt = ft[D_ATT:2 * D_ATT].astype(BF16)
    ones_rows = jnp.ones((DENOM_ROWS, TOK_TILE), BF16)
    for head in range(N_HEADS):
        vt_ref[0, head * VT_ROWS:head * VT_ROWS + HEAD_DIM, :] = vt[head * HEAD_DIM:(head + 1) * HEAD_DIM]
        vt_ref[0, head * VT_ROWS + HEAD_DIM:(head + 1) * VT_ROWS, :] = ones_rows

    half = D_ATT // 2
    r = lax.broadcasted_iota(jnp.int32, (half, half), 0) // HEAD_DIM
    c = lax.broadcasted_iota(jnp.int32, (half, half), 1) // HEAD_DIM
    same_head = r == c
    n_blocks = kbar_ref.shape[0]
    blk_row = lax.broadcasted_iota(jnp.int32, (n_blocks, TOK_TILE), 0).astype(F32)
    q_blk = (t * blocks_per_tile
             + lax.broadcasted_iota(jnp.int32, (n_blocks, TOK_TILE), 1) // MOBA_BLOCK).astype(F32)
    valid = blk_row < q_blk
    for quad in range(2):
        kq = kbar_ref[:, quad * half:(quad + 1) * half]
        kb = jnp.where(same_head, jnp.concatenate([kq] * 4, axis=0), 0.0)
        kb_hi, kb_lo = _split_bf16(kb)
        q_hi, q_lo = _split_bf16(qt[quad * half:(quad + 1) * half])
        gs = _dot(kb_hi, q_hi) + _dot(kb_hi, q_lo) + _dot(kb_lo, q_hi)
        for hq in range(4):
            g = jnp.where(valid, gs[hq * n_blocks:(hq + 1) * n_blocks], -jnp.inf)
            sel = jnp.zeros(g.shape, jnp.bool_)
            for _ in range(MOBA_TOPK):
                m = jnp.max(g, axis=0, keepdims=True)
                idx = jnp.min(jnp.where(g == m, blk_row, float(n_blocks)), axis=0, keepdims=True)
                pick = blk_row == idx
                sel = jnp.logical_or(sel, pick)
                g = jnp.where(pick, -jnp.inf, g)
            head = quad * 4 + hq
            bias_ref[0, head * n_blocks:(head + 1) * n_blocks, :] = jnp.where(
                jnp.logical_and(sel, valid), 0.0, NEG)


def _in_proj(x, g, wtok, wft):
    b, s, d = x.shape
    n_blocks = s // MOBA_BLOCK
    const = lambda shape: pl.BlockSpec(shape, lambda bi, ti: (0,) * len(shape))
    return pl.pallas_call(
        _in_proj_kernel,
        grid=(b, s // TOK_TILE),
        in_specs=[
            pl.BlockSpec((1, TOK_TILE, d), lambda bi, ti: (bi, ti, 0)),
            const((1, d)),
            const((d, 4 * D_ATT)), const((2 * D_ATT, d)),
        ],
        out_specs=[
            pl.BlockSpec((1, TOK_TILE, D_ATT), lambda bi, ti: (bi, ti, 0)),
            pl.BlockSpec((1, TOK_TILE, 3 * D_ATT), lambda bi, ti: (bi, ti, 0)),
            pl.BlockSpec((1, D_ATT, TOK_TILE), lambda bi, ti: (bi, 0, ti)),
            pl.BlockSpec((1, N_HEADS * n_blocks, TOK_TILE), lambda bi, ti: (bi, 0, ti)),
            pl.BlockSpec((1, N_HEADS * VT_ROWS, TOK_TILE), lambda bi, ti: (bi, 0, ti)),
        ],
        out_shape=[
            jax.ShapeDtypeStruct((b, s, D_ATT), BF16),
            jax.ShapeDtypeStruct((b, s, 3 * D_ATT), F32),
            jax.ShapeDtypeStruct((b, D_ATT, s), BF16),
            jax.ShapeDtypeStruct((b, N_HEADS * n_blocks, s), F32),
            jax.ShapeDtypeStruct((b, N_HEADS * VT_ROWS, s), BF16),
        ],
        scratch_shapes=[pltpu.VMEM((n_blocks, D_ATT), F32)],
        compiler_params=pltpu.CompilerParams(
            dimension_semantics=("arbitrary", "arbitrary"), vmem_limit_bytes=VMEM_LIMIT),
        name="in_proj",
    )(x, g, wtok, wft)


def _moba_steps(n_blocks):
    q_blocks = MOBA_QTILE // MOBA_BLOCK
    steps = [(0, 0, 1, 0)]
    for tile in range(1, n_blocks // q_blocks):
        n_past = tile * q_blocks - q_blocks
        steps.append((tile, n_past, 1, 0))
        for start in range(0, n_past, MOBA_GROUP):
            steps.append((tile, start, 0, 1 if start + MOBA_GROUP > n_past else 0))
    if len(steps) % 2 == 0:
        steps.append((steps[-1][0], steps[-1][1], 0, 2))
    return [np.asarray(col, np.int32) for col in zip(*steps)]


def _moba_kernel(tile_tab, start_tab, first_tab, off_tab, qt_ref, bias_ref, k_ref, vt_ref, o_ref, s_sc, p_sc):
    n_blocks = bias_ref.shape[1] // 2
    n_steps = tile_tab.shape[0]
    q_blocks = MOBA_QTILE // MOBA_BLOCK
    chunk_keys = MOBA_GROUP * MOBA_BLOCK
    future = (lax.broadcasted_iota(jnp.int32, (MOBA_BLOCK, MOBA_BLOCK), 0)
              > lax.broadcasted_iota(jnp.int32, (MOBA_BLOCK, MOBA_BLOCK), 1))
    qry_blk_row = lax.broadcasted_iota(jnp.int32, (1, MOBA_QTILE), 1) // MOBA_BLOCK
    zeros = jnp.zeros((HEAD_DIM, MOBA_QTILE), BF16)
    tail = tuple(range(MOBA_GROUP - q_blocks, MOBA_GROUP))

    def tile_lanes(n):
        return pl.ds(pl.multiple_of(tile_tab[n] * MOBA_QTILE, MOBA_QTILE), MOBA_QTILE)

    def chunk_keys_of(n):
        return pl.ds(pl.multiple_of(start_tab[n] * MOBA_BLOCK, q_blocks * MOBA_BLOCK), chunk_keys)

    def scores(n, slot, own_pos=tail):
        qt = qt_ref[0, :, tile_lanes(n)]
        qt_heads = (jnp.concatenate([qt[:HEAD_DIM], zeros], axis=0),
                    jnp.concatenate([zeros, qt[HEAD_DIM:]], axis=0))
        k_c = k_ref[0, chunk_keys_of(n), :]
        first = first_tab[n] == 1
        col_max = []
        for hd in range(2):
            s = _dot(k_c, qt_heads[hd])
            for r in range(MOBA_GROUP):
                s_r = s[r * MOBA_BLOCK:(r + 1) * MOBA_BLOCK]
                if r in own_pos:
                    cols = slice(own_pos.index(r) * MOBA_BLOCK, (own_pos.index(r) + 1) * MOBA_BLOCK)
                    parts = [s_r[:, :cols.start], s_r[:, cols] + jnp.where(jnp.logical_and(first, future), NEG, 0.0),
                             s_r[:, cols.stop:]]
                    s_r = jnp.concatenate([part for part in parts if part.shape[1]], axis=1)
                s_sc[slot, hd, r * MOBA_BLOCK:(r + 1) * MOBA_BLOCK, :] = s_r
                col_max.append(jnp.max(s_r, axis=0, keepdims=True))
        return tuple(col_max)

    def softmax(n, slot, m_pair, col_max, own_pos=tail):
        first = first_tab[n] == 1
        off = off_tab[n]
        alphas, m_news = [], []
        for hd in range(2):
            m_old = jnp.where(first, NEG, m_pair[hd])
            gates = []
            for r in range(MOBA_GROUP):
                gate = bias_ref[0, pl.ds(hd * n_blocks + start_tab[n] + r, 1), tile_lanes(n)]
                if r in own_pos:
                    gate = jnp.where(jnp.logical_and(first, qry_blk_row == own_pos.index(r)), 0.0, gate)
                gated_off = off >= 1 if r >= MOBA_GROUP - q_blocks else off >= 2
                gates.append(gate + jnp.where(gated_off, NEG, 0.0))
            m_new = m_old
            for r, gate in enumerate(gates):
                m_new = jnp.maximum(m_new, col_max[hd * MOBA_GROUP + r] + gate)
            alphas.append(jnp.exp2(m_old - m_new))
            m_news.append(m_new)
            for r, gate in enumerate(gates):
                rows = slice(r * MOBA_BLOCK, (r + 1) * MOBA_BLOCK)
                p_sc[slot, hd, rows, :] = jnp.exp2(s_sc[slot, hd, rows, :] - (m_new - gate)).astype(BF16)
        return tuple(alphas), tuple(m_news)

    def store(n, accs):
        ot = jnp.concatenate([a[:HEAD_DIM] * (1.0 / a[HEAD_DIM:HEAD_DIM + 1]) for a in accs], axis=0)
        o_ref[0, tile_lanes(n), :] = ot.T.astype(BF16)

    def accum(n, slot, alphas, accs):
        store(jnp.maximum(n - 1, 0), accs)
        vt_c = vt_ref[0, :, chunk_keys_of(n)]
        return tuple(alphas[hd] * accs[hd] + _dot(vt_c[hd * VT_ROWS:(hd + 1) * VT_ROWS], p_sc[slot, hd])
                     for hd in range(2))

    m0 = (jnp.full((1, MOBA_QTILE), NEG, F32),) * 2
    acc0 = (jnp.ones((VT_ROWS, MOBA_QTILE), F32),) * 2
    head = tuple(range(q_blocks))
    alpha0, m1 = softmax(0, 0, m0, scores(0, 0, own_pos=head), own_pos=head)
    cm1 = scores(1, 1)

    def two_steps(t, carry):
        alpha_prev, m, acc, cm_a = carry
        n_a = 2 * t + 1
        n_b = n_a + 1
        cm_b = scores(n_b, 0)
        alpha_a, m = softmax(n_a, 1, m, cm_a)
        acc = accum(n_a - 1, 0, alpha_prev, acc)
        cm_next = scores(jnp.minimum(n_b + 1, n_steps - 1), 1)
        alpha_b, m = softmax(n_b, 0, m, cm_b)
        acc = accum(n_a, 1, alpha_a, acc)
        return alpha_b, m, acc, cm_next

    alpha_l, _, acc, _ = lax.fori_loop(0, (n_steps - 1) // 2, two_steps, (alpha0, m1, acc0, cm1))
    store(n_steps - 1, accum(n_steps - 1, 0, alpha_l, acc))


def _moba(qt, bias, ka, vt):
    b, s, _ = ka.shape
    n_blocks = s // MOBA_BLOCK
    tables = _moba_steps(n_blocks)
    whole = lambda shape, index: pl.BlockSpec(shape, index, pipeline_mode=pl.Buffered(1))
    return pl.pallas_call(
        _moba_kernel,
        grid_spec=pltpu.PrefetchScalarGridSpec(
            num_scalar_prefetch=len(tables),
            grid=(b, N_PAIRS),
            in_specs=[
                whole((1, PAIR, s), lambda bi, p, *_: (bi, p, 0)),
                whole((1, 2 * n_blocks, s), lambda bi, p, *_: (bi, p, 0)),
                whole((1, s, PAIR), lambda bi, p, *_: (bi, 0, p)),
                whole((1, 2 * VT_ROWS, s), lambda bi, p, *_: (bi, p, 0)),
            ],
            out_specs=pl.BlockSpec((1, s, PAIR), lambda bi, p, *_: (bi, 0, p)),
            scratch_shapes=[pltpu.VMEM((2, 2, MOBA_GROUP * MOBA_BLOCK, MOBA_QTILE), F32),
                            pltpu.VMEM((2, 2, MOBA_GROUP * MOBA_BLOCK, MOBA_QTILE), BF16)],
        ),
        out_shape=jax.ShapeDtypeStruct((b, s, D_ATT), BF16),
        compiler_params=pltpu.CompilerParams(
            dimension_semantics=("arbitrary", "arbitrary"), vmem_limit_bytes=VMEM_LIMIT),
        name="moba",
    )(*tables, qt, bias, ka, vt)


def _dilated_kernel(q_ref, kc_ref, kp_ref, vc_ref, vp_ref, o_ref, o_sc, lse_sc, split_sc):
    first_tile = pl.program_id(2) == 0
    n_tiles = DIL_TILE // DIL_BAND
    head_a = lax.broadcasted_iota(jnp.int32, (DIL_TILE, PAIR), 1) < HEAD_DIM
    qi = lax.broadcasted_iota(jnp.int32, (DIL_BAND, 2 * DIL_BAND), 0)
    kj = lax.broadcasted_iota(jnp.int32, (DIL_BAND, 2 * DIL_BAND), 1)
    band = jnp.where(jnp.logical_and(kj >= qi, kj <= qi + DIL_BAND), 0.0, NEG)
    band_start = jnp.where(first_tile, jnp.where(kj >= DIL_BAND, band, NEG), band)
    operands = (q_ref, kc_ref, kp_ref, vc_ref, vp_ref)

    def residue_rows(a, dil, r, first, count):
        if dil > DIL_SPLIT:
            sub = dil // DIL_SPLIT
            base = (r % DIL_SPLIT) * (DIL_TILE // DIL_SPLIT) + r // DIL_SPLIT
            return split_sc[a, pl.ds(base + sub * first, count, stride=sub), :]
        return operands[a][0, pl.ds(first * dil + r, count, stride=dil), :]

    for pat, (window, dil) in enumerate(DIL_PATTERNS):
        rows = DIL_TILE // dil
        n_q = rows // DIL_BAND
        halo = rows - DIL_BAND
        q_parts, k_tiles, v_tiles, masks = [], [], [], []
        for r in range(dil):
            if dil == DIL_SPLIT:
                q_r, kc_r, kp_r, vc_r, vp_r = whole = [residue_rows(a, dil, r, 0, rows) for a in range(5)]
                for a in range(5):
                    split_sc[a, r * rows:(r + 1) * rows, :] = whole[a]
                kp_r, vp_r = kp_r[halo:], vp_r[halo:]
            else:
                q_r, kc_r, vc_r = (residue_rows(a, dil, r, 0, rows) for a in (0, 1, 3))
                kp_r, vp_r = (residue_rows(a, dil, r, halo, DIL_BAND) for a in (2, 4))
            q_parts.append(q_r)
            k_ext = jnp.concatenate([kp_r, kc_r], axis=0).astype(BF16)
            v_ext = jnp.concatenate([vp_r, vc_r], axis=0).astype(BF16)
            for qb in range(n_q):
                k_tiles.append(k_ext[qb * DIL_BAND:(qb + 2) * DIL_BAND])
                v_tiles.append(v_ext[qb * DIL_BAND:(qb + 2) * DIL_BAND])
                masks.append(band_start if qb == 0 else band)
        q_all = jnp.concatenate(q_parts, axis=0)
        mask_all = jnp.concatenate(masks, axis=0)
        outs, lses = [], []
        for hd in range(2):
            qh = jnp.where(head_a if hd == 0 else jnp.logical_not(head_a), q_all, 0.0).astype(BF16)
            s = jnp.concatenate(
                [_dot_nt(qh[n * DIL_BAND:(n + 1) * DIL_BAND], k_tiles[n]) for n in range(n_tiles)],
                axis=0) + mask_all
            m = jnp.max(s, axis=-1, keepdims=True)
            p = jnp.exp(s - m)
            l = jnp.sum(p, axis=-1, keepdims=True)
            pb = p.astype(BF16)
            o = jnp.concatenate(
                [_dot(pb[n * DIL_BAND:(n + 1) * DIL_BAND], v_tiles[n]) for n in range(n_tiles)], axis=0)
            outs.append(o / l)
            lses.append(m + jnp.log(l))
        o_pair = jnp.where(head_a, outs[0], outs[1])
        lse_pair = jnp.where(head_a, lses[0], lses[1])
        for r in range(dil):
            o_sc[pat, pl.ds(r, rows, stride=dil), :] = o_pair[r * rows:(r + 1) * rows]
            lse_sc[pat, pl.ds(r, rows, stride=dil), :] = lse_pair[r * rows:(r + 1) * rows]

    lse = lse_sc[...]
    w = jnp.exp(lse - jnp.max(lse, axis=0, keepdims=True))
    o_ref[0] = (jnp.sum(w * o_sc[...], axis=0) / jnp.sum(w, axis=0)).astype(BF16)


def _dilated(qkvb):
    b, s, _ = qkvb.shape
    tile_spec = lambda col0, prev: pl.BlockSpec(
        (1, DIL_TILE, PAIR),
        (lambda bi, p, t: (bi, jnp.maximum(t - 1, 0), col0 + p)) if prev else (lambda bi, p, t: (bi, t, col0 + p)))
    return pl.pallas_call(
        _dilated_kernel,
        grid=(b, N_PAIRS, s // DIL_TILE),
        in_specs=[tile_spec(0, False),
                  tile_spec(N_PAIRS, False), tile_spec(N_PAIRS, True),
                  tile_spec(2 * N_PAIRS, False), tile_spec(2 * N_PAIRS, True)],
        out_specs=pl.BlockSpec((1, DIL_TILE, PAIR), lambda bi, p, t: (bi, t, p)),
        out_shape=jax.ShapeDtypeStruct((b, s, D_ATT), BF16),
        scratch_shapes=[pltpu.VMEM((len(DIL_PATTERNS), DIL_TILE, PAIR), F32),
                        pltpu.VMEM((len(DIL_PATTERNS), DIL_TILE, PAIR), F32),
                        pltpu.VMEM((5, DIL_TILE, PAIR), F32)],
        compiler_params=pltpu.CompilerParams(
            dimension_semantics=("arbitrary", "arbitrary", "arbitrary"), vmem_limit_bytes=VMEM_LIMIT),
        name="dilated",
    )(qkvb, qkvb, qkvb, qkvb, qkvb)


def _merge_kernel(x_ref, xh_ref, oa_ref, ob_ref, gpre_ref, gpost_ref, wu_ref, wg_ref, wa_ref, wb_ref, wc_ref,
                  wout_ref, wpool_ref, pscale_ref, o_ref):
    t = pl.program_id(1)
    x = x_ref[0]
    h = _rms(x, gpre_ref[...]).astype(BF16)
    hh = _rms(xh_ref[0], gpre_ref[...]).astype(BF16)
    u = _dot(h, wu_ref[...])
    uh = jnp.where(t == 0, 0.0, _dot(hh, wu_ref[...]))
    u_ext = jnp.concatenate([uh, u], axis=0)
    pos = (t * TOK_TILE + lax.broadcasted_iota(jnp.int32, (TOK_TILE, POOL_GROUP_DIM), 0) + 1).astype(F32)
    oc = []
    for g, w in enumerate(POOL_WINDOWS):
        cols = slice(g * POOL_GROUP_DIM, (g + 1) * POOL_GROUP_DIM)
        acc = u_ext[:, cols]
        span = 1
        while span < w:
            acc = acc + pltpu.roll(acc, span, 0)
            span *= 2
        pooled = acc[POOL_HALO:] / jnp.minimum(pos, float(w)) - u[:, cols]
        oc.append(_dot(pooled.astype(BF16), wpool_ref[g]))
    oc = (jnp.concatenate(oc, axis=1) * pscale_ref[...]).astype(BF16)

    merged = None
    for br, (src, w_ref) in enumerate(((oa_ref[0], wa_ref), (ob_ref[0], wb_ref), (oc, wc_ref))):
        gz = _dot(h, wg_ref[:, br * D_MODEL:(br + 1) * D_MODEL])
        term = (1.0 / (1.0 + jnp.exp(-gz))) * _dot(src, w_ref[...])
        merged = term if merged is None else merged + term
    y = _dot(merged.astype(BF16), wout_ref[...])
    o_ref[0] = x + _rms(y, gpost_ref[...])


def _merge(x, oa, ob, gpre, gpost, wu, wg, wa, wb, wc, wout, wpool, pscale):
    b, s, d = x.shape
    const = lambda shape: pl.BlockSpec(shape, lambda bi, ti: (0,) * len(shape))
    halo_blocks = TOK_TILE // POOL_HALO
    return pl.pallas_call(
        _merge_kernel,
        grid=(b, s // TOK_TILE),
        in_specs=[
            pl.BlockSpec((1, TOK_TILE, d), lambda bi, ti: (bi, ti, 0)),
            pl.BlockSpec((1, POOL_HALO, d), lambda bi, ti: (bi, jnp.maximum(ti * halo_blocks - 1, 0), 0)),
            pl.BlockSpec((1, TOK_TILE, D_ATT), lambda bi, ti: (bi, ti, 0)),
            pl.BlockSpec((1, TOK_TILE, D_ATT), lambda bi, ti: (bi, ti, 0)),
            const((1, d)), const((1, d)),
            const((d, POOL_WIDTH)), const((d, 3 * d)),
            const((D_ATT, d)), const((D_ATT, d)), const((POOL_WIDTH, d)),
            const((d, d)), const((len(POOL_WINDOWS), POOL_GROUP_DIM, POOL_GROUP_DIM)), const((1, POOL_WIDTH)),
        ],
        out_specs=pl.BlockSpec((1, TOK_TILE, d), lambda bi, ti: (bi, ti, 0)),
        out_shape=jax.ShapeDtypeStruct((b, s, d), F32),
        compiler_params=pltpu.CompilerParams(
            dimension_semantics=("arbitrary", "arbitrary"), vmem_limit_bytes=VMEM_LIMIT),
        name="merge",
    )(x, x, oa, ob, gpre, gpost, wu, wg, wa, wb, wc, wout, wpool, pscale)


def _mlp_kernel(x_ref, gpre_ref, gpost_ref, w1_ref, w2_ref, o_ref):
    x = x_ref[0]
    h = _rms(x, gpre_ref[...]).astype(BF16)
    y = None
    for c in range(D_FF // D_MODEL):
        cols = slice(c * D_MODEL, (c + 1) * D_MODEL)
        a = jnp.maximum(_dot(h, w1_ref[:, cols]), 0.0)
        part = _dot((a * a).astype(BF16), w2_ref[cols, :])
        y = part if y is None else y + part
    o_ref[0] = x + _rms(y, gpost_ref[...])


def _mlp(x, gpre, gpost, w1, w2):
    b, s, d = x.shape
    const = lambda shape: pl.BlockSpec(shape, lambda bi, ti: (0,) * len(shape))
    return pl.pallas_call(
        _mlp_kernel,
        grid=(b, s // TOK_TILE),
        in_specs=[pl.BlockSpec((1, TOK_TILE, d), lambda bi, ti: (bi, ti, 0)),
                  const((1, d)), const((1, d)), const((d, D_FF)), const((D_FF, d))],
        out_specs=pl.BlockSpec((1, TOK_TILE, d), lambda bi, ti: (bi, ti, 0)),
        out_shape=jax.ShapeDtypeStruct((b, s, d), F32),
        compiler_params=pltpu.CompilerParams(
            dimension_semantics=("arbitrary", "arbitrary"), vmem_limit_bytes=VMEM_LIMIT),
        name="mlp",
    )(x, gpre, gpost, w1, w2)


def _layer(x, w_in, w_br_a, w_br_b, w_br_c, w_out, w_pool, pool_scale,
           g_pre_mix, g_post_mix, g_pre_mlp, g_post_mlp, w_ff1, w_ff2):
    col = lambda k: slice(k * D_ATT, (k + 1) * D_ATT)
    wtok = w_in[:, D_ATT:6 * D_ATT]
    wtok = jnp.concatenate([wtok[:, :D_ATT], wtok[:, 2 * D_ATT:]], axis=1).astype(BF16)
    wft = jnp.concatenate([w_in[:, col(0)], w_in[:, col(2)]], axis=1).T.astype(BF16)
    wu = w_in[:, 6 * D_ATT:6 * D_ATT + POOL_WIDTH].astype(BF16)
    wg = w_in[:, 6 * D_ATT + POOL_WIDTH:].astype(BF16)
    row = lambda v: v.reshape(1, -1)

    ka, qkvb, qt, bias, vt = _in_proj(x, row(g_pre_mix), wtok, wft)
    oa = _moba(qt, bias, ka, vt)
    ob = _dilated(qkvb)
    x = _merge(x, oa, ob, row(g_pre_mix), row(g_post_mix), wu, wg,
               w_br_a.astype(BF16), w_br_b.astype(BF16), w_br_c.astype(BF16), w_out.astype(BF16),
               w_pool.astype(BF16), row(pool_scale))
    return _mlp(x, row(g_pre_mlp), row(g_post_mlp), w_ff1.astype(BF16), w_ff2.astype(BF16))


@jax.jit
def kernel(x, w_in, w_br_a, w_br_b, w_br_c, w_out, w_pool, pool_scale, g_pre_mix, g_post_mix, g_pre_mlp, g_post_mlp, w_ff1, w_ff2):
    for l in range(w_in.shape[0]):
        x = _layer(x, w_in[l], w_br_a[l], w_br_b[l], w_br_c[l], w_out[l], w_pool[l], pool_scale[l],
                   g_pre_mix[l], g_post_mix[l], g_pre_mlp[l], g_post_mlp[l], w_ff1[l], w_ff2[l])
    return x
```

```python
import jax
import jax.numpy as jnp
import numpy as np
from jax import lax
from jax.experimental import pallas as pl
from jax.experimental.pallas import tpu as pltpu

F32 = jnp.float32
BF16 = jnp.bfloat16

D_MODEL = 1024
HEAD_DIM = 64
N_HEADS = 8
D_ATT = N_HEADS * HEAD_DIM
PAIR = 2 * HEAD_DIM
N_PAIRS = N_HEADS // 2
MOBA_BLOCK = 256
MOBA_TOPK = 3
MOBA_GROUP = 4
MOBA_QTILE = 512
DENOM_ROWS = 16
VT_ROWS = HEAD_DIM + DENOM_ROWS
LOG2_E = 1.4426950408889634
DIL_PATTERNS = ((128, 1), (512, 4), (2048, 16))
DIL_BAND = 128
DIL_TILE = 2048
DIL_SPLIT = 4
POOL_WINDOWS = (2, 4, 8, 16)
POOL_WIDTH = 512
POOL_GROUP_DIM = 128
POOL_HALO = 16
D_FF = 4 * D_MODEL
RMS_EPS = 1e-6
QK_SCALE = HEAD_DIM ** -0.5
NEG = -1e30

TOK_TILE = 1024
VMEM_LIMIT = 56 * 1024 * 1024

NT_DIMS = (((1,), (1,)), ((), ()))


def _dot(a, b):
    return jnp.dot(a, b, preferred_element_type=F32)


def _dot_nt(a, b):
    return lax.dot_general(a, b, NT_DIMS, preferred_element_type=F32)


def _split_bf16(v):
    hi = v.astype(BF16)
    lo = (v - hi.astype(F32)).astype(BF16)
    return hi, lo


def _rms(x, g):
    var = jnp.mean(x * x, axis=-1, keepdims=True)
    return x * lax.rsqrt(var + RMS_EPS) * g


def _in_proj_kernel(x_ref, g_ref, wtok_ref, wft_ref, ka_ref, qkvb_ref, qt_ref, bias_ref, vt_ref, kbar_ref):
    t = pl.program_id(1)
    blocks_per_tile = TOK_TILE // MOBA_BLOCK

    @pl.when(t == 0)
    def _():
        kbar_ref[...] = jnp.zeros_like(kbar_ref)

    h = _rms(x_ref[0], g_ref[...]).astype(BF16)

    tok = _dot(h, wtok_ref[...])
    ka = tok[:, 0:D_ATT]
    ka_ref[0] = ka.astype(BF16)
    for blk in range(blocks_per_tile):
        kbar_ref[pl.ds(t * blocks_per_tile + blk, 1), :] = jnp.mean(
            ka[blk * MOBA_BLOCK:(blk + 1) * MOBA_BLOCK], axis=0, keepdims=True)
    qkvb_ref[0, :, 0:D_ATT] = tok[:, D_ATT:2 * D_ATT] * QK_SCALE
    qkvb_ref[0, :, D_ATT:3 * D_ATT] = tok[:, 2 * D_ATT:4 * D_ATT]

    ft = _dot_nt(wft_ref[...], h)
    qt = ft[0:D_ATT] * (QK_SCALE * LOG2_E)
    qt_ref[0] = qt.astype(BF16)
    vt = ft[D_ATT:2 * D_ATT].astype(BF16)
    ones_rows = jnp.ones((DENOM_ROWS, TOK_TILE), BF16)
    for head in range(N_HEADS):
        vt_ref[0, head * VT_ROWS:head * VT_ROWS + HEAD_DIM, :] = vt[head * HEAD_DIM:(head + 1) * HEAD_DIM]
        vt_ref[0, head * VT_ROWS + HEAD_DIM:(head + 1) * VT_ROWS, :] = ones_rows

    half = D_ATT // 2
    r = lax.broadcasted_iota(jnp.int32, (half, half), 0) // HEAD_DIM
    c = lax.broadcasted_iota(jnp.int32, (half, half), 1) // HEAD_DIM
    same_head = r == c
    n_blocks = kbar_ref.shape[0]
    blk_row = lax.broadcasted_iota(jnp.int32, (n_blocks, TOK_TILE), 0).astype(F32)
    q_blk = (t * blocks_per_tile
             + lax.broadcasted_iota(jnp.int32, (n_blocks, TOK_TILE), 1) // MOBA_BLOCK).astype(F32)
    valid = blk_row < q_blk
    for quad in range(2):
        kq = kbar_ref[:, quad * half:(quad + 1) * half]
        kb = jnp.where(same_head, jnp.concatenate([kq] * 4, axis=0), 0.0)
        kb_hi, kb_lo = _split_bf16(kb)
        q_hi, q_lo = _split_bf16(qt[quad * half:(quad + 1) * half])
        gs = _dot(kb_hi, q_hi) + _dot(kb_hi, q_lo) + _dot(kb_lo, q_hi)
        for hq in range(4):
            g = jnp.where(valid, gs[hq * n_blocks:(hq + 1) * n_blocks], -jnp.inf)
            sel = jnp.zeros(g.shape, jnp.bool_)
            for _ in range(MOBA_TOPK):
                m = jnp.max(g, axis=0, keepdims=True)
                idx = jnp.min(jnp.where(g == m, blk_row, float(n_blocks)), axis=0, keepdims=True)
                pick = blk_row == idx
                sel = jnp.logical_or(sel, pick)
                g = jnp.where(pick, -jnp.inf, g)
            head = quad * 4 + hq
            bias_ref[0, head * n_blocks:(head + 1) * n_blocks, :] = jnp.where(
                jnp.logical_and(sel, valid), 0.0, NEG)


def _in_proj(x, g, wtok, wft):
    b, s, d = x.shape
    n_blocks = s // MOBA_BLOCK
    const = lambda shape: pl.BlockSpec(shape, lambda bi, ti: (0,) * len(shape), pipeline_mode=pl.Buffered(1))
    return pl.pallas_call(
        _in_proj_kernel,
        grid=(b, s // TOK_TILE),
        in_specs=[
            pl.BlockSpec((1, TOK_TILE, d), lambda bi, ti: (bi, ti, 0)),
            const((1, d)),
            const((d, 4 * D_ATT)), const((2 * D_ATT, d)),
        ],
        out_specs=[
            pl.BlockSpec((1, TOK_TILE, D_ATT), lambda bi, ti: (bi, ti, 0)),
            pl.BlockSpec((1, TOK_TILE, 3 * D_ATT), lambda bi, ti: (bi, ti, 0)),
            pl.BlockSpec((1, D_ATT, TOK_TILE), lambda bi, ti: (bi, 0, ti)),
            pl.BlockSpec((1, N_HEADS * n_blocks, TOK_TILE), lambda bi, ti: (bi, 0, ti)),
            pl.BlockSpec((1, N_HEADS * VT_ROWS, TOK_TILE), lambda bi, ti: (bi, 0, ti)),
        ],
        out_shape=[
            jax.ShapeDtypeStruct((b, s, D_ATT), BF16),
            jax.ShapeDtypeStruct((b, s, 3 * D_ATT), F32),
            jax.ShapeDtypeStruct((b, D_ATT, s), BF16),
            jax.ShapeDtypeStruct((b, N_HEADS * n_blocks, s), F32),
            jax.ShapeDtypeStruct((b, N_HEADS * VT_ROWS, s), BF16),
        ],
        scratch_shapes=[pltpu.VMEM((n_blocks, D_ATT), F32)],
        compiler_params=pltpu.CompilerParams(
            dimension_semantics=("arbitrary", "arbitrary"), vmem_limit_bytes=VMEM_LIMIT),
        name="in_proj",
    )(x, g, wtok, wft)


def _moba_steps(n_blocks):
    q_blocks = MOBA_QTILE // MOBA_BLOCK
    steps = [(0, 0, 1, 0)]
    for tile in range(1, n_blocks // q_blocks):
        n_past = tile * q_blocks - q_blocks
        steps.append((tile, n_past, 1, 0))
        for start in range(0, n_past, MOBA_GROUP):
            steps.append((tile, start, 0, 1 if start + MOBA_GROUP > n_past else 0))
    if len(steps) % 2 == 0:
        steps.append((steps[-1][0], steps[-1][1], 0, 2))
    return [np.asarray(col, np.int32) for col in zip(*steps)]


def _moba_kernel(tile_tab, start_tab, first_tab, off_tab, qt_ref, bias_ref, k_ref, vt_ref, o_ref, s_sc, p_sc):
    n_blocks = bias_ref.shape[1] // 2
    n_steps = tile_tab.shape[0]
    q_blocks = MOBA_QTILE // MOBA_BLOCK
    chunk_keys = MOBA_GROUP * MOBA_BLOCK
    future = (lax.broadcasted_iota(jnp.int32, (MOBA_BLOCK, MOBA_BLOCK), 0)
              > lax.broadcasted_iota(jnp.int32, (MOBA_BLOCK, MOBA_BLOCK), 1))
    qry_blk_row = lax.broadcasted_iota(jnp.int32, (1, MOBA_QTILE), 1) // MOBA_BLOCK
    zeros = jnp.zeros((HEAD_DIM, MOBA_QTILE), BF16)
    tail = tuple(range(MOBA_GROUP - q_blocks, MOBA_GROUP))

    def tile_lanes(n):
        return pl.ds(pl.multiple_of(tile_tab[n] * MOBA_QTILE, MOBA_QTILE), MOBA_QTILE)

    def chunk_keys_of(n):
        return pl.ds(pl.multiple_of(start_tab[n] * MOBA_BLOCK, q_blocks * MOBA_BLOCK), chunk_keys)

    def scores(n, slot, own_pos=tail):
        qt = qt_ref[0, :, tile_lanes(n)]
        qt_heads = (jnp.concatenate([qt[:HEAD_DIM], zeros], axis=0),
                    jnp.concatenate([zeros, qt[HEAD_DIM:]], axis=0))
        k_c = k_ref[0, chunk_keys_of(n), :]
        first = first_tab[n] == 1
        col_max = []
        for hd in range(2):
            s = _dot(k_c, qt_heads[hd])
            for r in range(MOBA_GROUP):
                s_r = s[r * MOBA_BLOCK:(r + 1) * MOBA_BLOCK]
                if r in own_pos:
                    cols = slice(own_pos.index(r) * MOBA_BLOCK, (own_pos.index(r) + 1) * MOBA_BLOCK)
                    parts = [s_r[:, :cols.start], s_r[:, cols] + jnp.where(jnp.logical_and(first, future), NEG, 0.0),
                             s_r[:, cols.stop:]]
                    s_r = jnp.concatenate([part for part in parts if part.shape[1]], axis=1)
                s_sc[slot, hd, r * MOBA_BLOCK:(r + 1) * MOBA_BLOCK, :] = s_r
                col_max.append(jnp.max(s_r, axis=0, keepdims=True))
        return tuple(col_max)

    def softmax(n, slot, m_pair, col_max, own_pos=tail):
        first = first_tab[n] == 1
        off = off_tab[n]
        alphas, m_news = [], []
        for hd in range(2):
            m_old = jnp.where(first, NEG, m_pair[hd])
            gates = []
            for r in range(MOBA_GROUP):
                gate = bias_ref[0, pl.ds(hd * n_blocks + start_tab[n] + r, 1), tile_lanes(n)]
                if r in own_pos:
                    gate = jnp.where(jnp.logical_and(first, qry_blk_row == own_pos.index(r)), 0.0, gate)
                gated_off = off >= 1 if r >= MOBA_GROUP - q_blocks else off >= 2
                gates.append(gate + jnp.where(gated_off, NEG, 0.0))
            m_new = m_old
            for r, gate in enumerate(gates):
                m_new = jnp.maximum(m_new, col_max[hd * MOBA_GROUP + r] + gate)
            alphas.append(jnp.exp2(m_old - m_new))
            m_news.append(m_new)
            for r, gate in enumerate(gates):
                rows = slice(r * MOBA_BLOCK, (r + 1) * MOBA_BLOCK)
                p_sc[slot, hd, rows, :] = jnp.exp2(s_sc[slot, hd, rows, :] - (m_new - gate)).astype(BF16)
        return tuple(alphas), tuple(m_news)

    def store(n, accs):
        ot = jnp.concatenate([a[:HEAD_DIM] * (1.0 / a[HEAD_DIM:HEAD_DIM + 1]) for a in accs], axis=0)
        o_ref[0, tile_lanes(n), :] = ot.T.astype(BF16)

    def accum(n, slot, alphas, accs):
        store(jnp.maximum(n - 1, 0), accs)
        vt_c = vt_ref[0, :, chunk_keys_of(n)]
        return tuple(alphas[hd] * accs[hd] + _dot(vt_c[hd * VT_ROWS:(hd + 1) * VT_ROWS], p_sc[slot, hd])
                     for hd in range(2))

    m0 = (jnp.full((1, MOBA_QTILE), NEG, F32),) * 2
    acc0 = (jnp.ones((VT_ROWS, MOBA_QTILE), F32),) * 2
    head = tuple(range(q_blocks))
    alpha0, m1 = softmax(0, 0, m0, scores(0, 0, own_pos=head), own_pos=head)
    cm1 = scores(1, 1)

    def two_steps(t, carry):
        alpha_prev, m, acc, cm_a = carry
        n_a = 2 * t + 1
        n_b = n_a + 1
        cm_b = scores(n_b, 0)
        alpha_a, m = softmax(n_a, 1, m, cm_a)
        acc = accum(n_a - 1, 0, alpha_prev, acc)
        cm_next = scores(jnp.minimum(n_b + 1, n_steps - 1), 1)
        alpha_b, m = softmax(n_b, 0, m, cm_b)
        acc = accum(n_a, 1, alpha_a, acc)
        return alpha_b, m, acc, cm_next

    alpha_l, _, acc, _ = lax.fori_loop(0, (n_steps - 1) // 2, two_steps, (alpha0, m1, acc0, cm1))
    store(n_steps - 1, accum(n_steps - 1, 0, alpha_l, acc))


def _moba(qt, bias, ka, vt):
    b, s, _ = ka.shape
    n_blocks = s // MOBA_BLOCK
    tables = _moba_steps(n_blocks)
    whole = lambda shape, index: pl.BlockSpec(shape, index, pipeline_mode=pl.Buffered(1))
    return pl.pallas_call(
        _moba_kernel,
        grid_spec=pltpu.PrefetchScalarGridSpec(
            num_scalar_prefetch=len(tables),
            grid=(b, N_PAIRS),
            in_specs=[
                whole((1, PAIR, s), lambda bi, p, *_: (bi, p, 0)),
                whole((1, 2 * n_blocks, s), lambda bi, p, *_: (bi, p, 0)),
                whole((1, s, PAIR), lambda bi, p, *_: (bi, 0, p)),
                whole((1, 2 * VT_ROWS, s), lambda bi, p, *_: (bi, p, 0)),
            ],
            out_specs=pl.BlockSpec((1, s, PAIR), lambda bi, p, *_: (bi, 0, p)),
            scratch_shapes=[pltpu.VMEM((2, 2, MOBA_GROUP * MOBA_BLOCK, MOBA_QTILE), F32),
                            pltpu.VMEM((2, 2, MOBA_GROUP * MOBA_BLOCK, MOBA_QTILE), BF16)],
        ),
        out_shape=jax.ShapeDtypeStruct((b, s, D_ATT), BF16),
        compiler_params=pltpu.CompilerParams(
            dimension_semantics=("arbitrary", "arbitrary"), vmem_limit_bytes=VMEM_LIMIT),
        name="moba",
    )(*tables, qt, bias, ka, vt)


def _dilated_kernel(q_ref, kc_ref, kp_ref, vc_ref, vp_ref, o_ref, o_sc, lse_sc, split_sc):
    first_tile = pl.program_id(2) == 0
    n_tiles = DIL_TILE // DIL_BAND
    head_a = lax.broadcasted_iota(jnp.int32, (DIL_TILE, PAIR), 1) < HEAD_DIM
    qi = lax.broadcasted_iota(jnp.int32, (DIL_BAND, 2 * DIL_BAND), 0)
    kj = lax.broadcasted_iota(jnp.int32, (DIL_BAND, 2 * DIL_BAND), 1)
    band = jnp.where(jnp.logical_and(kj >= qi, kj <= qi + DIL_BAND), 0.0, NEG)
    band_start = jnp.where(first_tile, jnp.where(kj >= DIL_BAND, band, NEG), band)
    operands = (q_ref, kc_ref, kp_ref, vc_ref, vp_ref)

    def residue_rows(a, dil, r, first, count):
        if dil > DIL_SPLIT:
            sub = dil // DIL_SPLIT
            base = (r % DIL_SPLIT) * (DIL_TILE // DIL_SPLIT) + r // DIL_SPLIT
            return split_sc[a, pl.ds(base + sub * first, count, stride=sub), :]
        return operands[a][0, pl.ds(first * dil + r, count, stride=dil), :]

    for pat, (window, dil) in enumerate(DIL_PATTERNS):
        rows = DIL_TILE // dil
        n_q = rows // DIL_BAND
        halo = rows - DIL_BAND
        q_parts, k_tiles, v_tiles, masks = [], [], [], []
        for r in range(dil):
            if dil == DIL_SPLIT:
                q_r, kc_r, kp_r, vc_r, vp_r = whole = [residue_rows(a, dil, r, 0, rows) for a in range(5)]
                for a in range(5):
                    split_sc[a, r * rows:(r + 1) * rows, :] = whole[a]
                kp_r, vp_r = kp_r[halo:], vp_r[halo:]
            else:
                q_r, kc_r, vc_r = (residue_rows(a, dil, r, 0, rows) for a in (0, 1, 3))
                kp_r, vp_r = (residue_rows(a, dil, r, halo, DIL_BAND) for a in (2, 4))
            q_parts.append(q_r)
            k_ext = jnp.concatenate([kp_r, kc_r], axis=0).astype(BF16)
            v_ext = jnp.concatenate([vp_r, vc_r], axis=0).astype(BF16)
            for qb in range(n_q):
                k_tiles.append(k_ext[qb * DIL_BAND:(qb + 2) * DIL_BAND])
                v_tiles.append(v_ext[qb * DIL_BAND:(qb + 2) * DIL_BAND])
                masks.append(band_start if qb == 0 else band)
        q_all = jnp.concatenate(q_parts, axis=0)
        mask_all = jnp.concatenate(masks, axis=0)
        outs, lses = [], []
        for hd in range(2):
            qh = jnp.where(head_a if hd == 0 else jnp.logical_not(head_a), q_all, 0.0).astype(BF16)
            s = jnp.concatenate(
                [_dot_nt(qh[n * DIL_BAND:(n + 1) * DIL_BAND], k_tiles[n]) for n in range(n_tiles)],
                axis=0) + mask_all
            m = jnp.max(s, axis=-1, keepdims=True)
            p = jnp.exp(s - m)
            l = jnp.sum(p, axis=-1, keepdims=True)
            pb = p.astype(BF16)
            o = jnp.concatenate(
                [_dot(pb[n * DIL_BAND:(n + 1) * DIL_BAND], v_tiles[n]) for n in range(n_tiles)], axis=0)
            outs.append(o / l)
            lses.append(m + jnp.log(l))
        o_pair = jnp.where(head_a, outs[0], outs[1])
        lse_pair = jnp.where(head_a, lses[0], lses[1])
        for r in range(dil):
            o_sc[pat, pl.ds(r, rows, stride=dil), :] = o_pair[r * rows:(r + 1) * rows]
            lse_sc[pat, pl.ds(r, rows, stride=dil), :] = lse_pair[r * rows:(r + 1) * rows]

    lse = lse_sc[...]
    w = jnp.exp(lse - jnp.max(lse, axis=0, keepdims=True))
    o_ref[0] = (jnp.sum(w * o_sc[...], axis=0) / jnp.sum(w, axis=0)).astype(BF16)


def _dilated(qkvb):
    b, s, _ = qkvb.shape
    tile_spec = lambda col0, prev: pl.BlockSpec(
        (1, DIL_TILE, PAIR),
        (lambda bi, p, t: (bi, jnp.maximum(t - 1, 0), col0 + p)) if prev else (lambda bi, p, t: (bi, t, col0 + p)))
    return pl.pallas_call(
        _dilated_kernel,
        grid=(b, N_PAIRS, s // DIL_TILE),
        in_specs=[tile_spec(0, False),
                  tile_spec(N_PAIRS, False), tile_spec(N_PAIRS, True),
                  tile_spec(2 * N_PAIRS, False), tile_spec(2 * N_PAIRS, True)],
        out_specs=pl.BlockSpec((1, DIL_TILE, PAIR), lambda bi, p, t: (bi, t, p)),
        out_shape=jax.ShapeDtypeStruct((b, s, D_ATT), BF16),
        scratch_shapes=[pltpu.VMEM((len(DIL_PATTERNS), DIL_TILE, PAIR), F32),
                        pltpu.VMEM((len(DIL_PATTERNS), DIL_TILE, PAIR), F32),
                        pltpu.VMEM((5, DIL_TILE, PAIR), F32)],
        compiler_params=pltpu.CompilerParams(
            dimension_semantics=("arbitrary", "arbitrary", "arbitrary"), vmem_limit_bytes=VMEM_LIMIT),
        name="dilated",
    )(qkvb, qkvb, qkvb, qkvb, qkvb)


def _merge_kernel(x_ref, xh_ref, oa_ref, ob_ref, gpre_ref, gpost_ref, wu_ref, wg_ref, wa_ref, wb_ref, wc_ref,
                  wout_ref, wpool_ref, pscale_ref, o_ref):
    t = pl.program_id(1)
    x = x_ref[0]
    h = _rms(x, gpre_ref[...]).astype(BF16)
    hh = _rms(xh_ref[0], gpre_ref[...]).astype(BF16)
    u = _dot(h, wu_ref[...])
    uh = jnp.where(t == 0, 0.0, _dot(hh, wu_ref[...]))
    u_ext = jnp.concatenate([uh, u], axis=0)
    pos = (t * TOK_TILE + lax.broadcasted_iota(jnp.int32, (TOK_TILE, POOL_GROUP_DIM), 0) + 1).astype(F32)
    oc = []
    for g, w in enumerate(POOL_WINDOWS):
        cols = slice(g * POOL_GROUP_DIM, (g + 1) * POOL_GROUP_DIM)
        acc = u_ext[:, cols]
        span = 1
        while span < w:
            acc = acc + pltpu.roll(acc, span, 0)
            span *= 2
        pooled = acc[POOL_HALO:] / jnp.minimum(pos, float(w)) - u[:, cols]
        oc.append(_dot(pooled.astype(BF16), wpool_ref[g]))
    oc = (jnp.concatenate(oc, axis=1) * pscale_ref[...]).astype(BF16)

    merged = None
    for br, (src, w_ref) in enumerate(((oa_ref[0], wa_ref), (ob_ref[0], wb_ref), (oc, wc_ref))):
        gz = _dot(h, wg_ref[:, br * D_MODEL:(br + 1) * D_MODEL])
        term = (1.0 / (1.0 + jnp.exp(-gz))) * _dot(src, w_ref[...])
        merged = term if merged is None else merged + term
    y = _dot(merged.astype(BF16), wout_ref[...])
    o_ref[0] = x + _rms(y, gpost_ref[...])


def _merge(x, oa, ob, gpre, gpost, wu, wg, wa, wb, wc, wout, wpool, pscale):
    b, s, d = x.shape
    const = lambda shape: pl.BlockSpec(shape, lambda bi, ti: (0,) * len(shape), pipeline_mode=pl.Buffered(1))
    halo_blocks = TOK_TILE // POOL_HALO
    return pl.pallas_call(
        _merge_kernel,
        grid=(b, s // TOK_TILE),
        in_specs=[
            pl.BlockSpec((1, TOK_TILE, d), lambda bi, ti: (bi, ti, 0)),
            pl.BlockSpec((1, POOL_HALO, d), lambda bi, ti: (bi, jnp.maximum(ti * halo_blocks - 1, 0), 0)),
            pl.BlockSpec((1, TOK_TILE, D_ATT), lambda bi, ti: (bi, ti, 0)),
            pl.BlockSpec((1, TOK_TILE, D_ATT), lambda bi, ti: (bi, ti, 0)),
            const((1, d)), const((1, d)),
            const((d, POOL_WIDTH)), const((d, 3 * d)),
            const((D_ATT, d)), const((D_ATT, d)), const((POOL_WIDTH, d)),
            const((d, d)), const((len(POOL_WINDOWS), POOL_GROUP_DIM, POOL_GROUP_DIM)), const((1, POOL_WIDTH)),
        ],
        out_specs=pl.BlockSpec((1, TOK_TILE, d), lambda bi, ti: (bi, ti, 0)),
        out_shape=jax.ShapeDtypeStruct((b, s, d), F32),
        compiler_params=pltpu.CompilerParams(
            dimension_semantics=("arbitrary", "arbitrary"), vmem_limit_bytes=VMEM_LIMIT),
        name="merge",
    )(x, x, oa, ob, gpre, gpost, wu, wg, wa, wb, wc, wout, wpool, pscale)


def _mlp_kernel(x_ref, gpre_ref, gpost_ref, w1_ref, w2_ref, o_ref):
    x = x_ref[0]
    h = _rms(x, gpre_ref[...]).astype(BF16)
    y = None
    for c in range(D_FF // D_MODEL):
        cols = slice(c * D_MODEL, (c + 1) * D_MODEL)
        a = jnp.maximum(_dot(h, w1_ref[:, cols]), 0.0)
        part = _dot((a * a).astype(BF16), w2_ref[cols, :])
        y = part if y is None else y + part
    o_ref[0] = x + _rms(y, gpost_ref[...])


def _mlp(x, gpre, gpost, w1, w2):
    b, s, d = x.shape
    const = lambda shape: pl.BlockSpec(shape, lambda bi, ti: (0,) * len(shape), pipeline_mode=pl.Buffered(1))
    return pl.pallas_call(
        _mlp_kernel,
        grid=(b, s // TOK_TILE),
        in_specs=[pl.BlockSpec((1, TOK_TILE, d), lambda bi, ti: (bi, ti, 0)),
                  const((1, d)), const((1, d)), const((d, D_FF)), const((D_FF, d))],
        out_specs=pl.BlockSpec((1, TOK_TILE, d), lambda bi, ti: (bi, ti, 0)),
        out_shape=jax.ShapeDtypeStruct((b, s, d), F32),
        compiler_params=pltpu.CompilerParams(
            dimension_semantics=("arbitrary", "arbitrary"), vmem_limit_bytes=VMEM_LIMIT),
        name="mlp",
    )(x, gpre, gpost, w1, w2)


def _layer(x, w_in, w_br_a, w_br_b, w_br_c, w_out, w_pool, pool_scale,
           g_pre_mix, g_post_mix, g_pre_mlp, g_post_mlp, w_ff1, w_ff2):
    col = lambda k: slice(k * D_ATT, (k + 1) * D_ATT)
    wtok = w_in[:, D_ATT:6 * D_ATT]
    wtok = jnp.concatenate([wtok[:, :D_ATT], wtok[:, 2 * D_ATT:]], axis=1).astype(BF16)
    wft = jnp.concatenate([w_in[:, col(0)], w_in[:, col(2)]], axis=1).T.astype(BF16)
    wu = w_in[:, 6 * D_ATT:6 * D_ATT + POOL_WIDTH].astype(BF16)
    wg = w_in[:, 6 * D_ATT + POOL_WIDTH:].astype(BF16)
    row = lambda v: v.reshape(1, -1)

    ka, qkvb, qt, bias, vt = _in_proj(x, row(g_pre_mix), wtok, wft)
    oa = _moba(qt, bias, ka, vt)
    ob = _dilated(qkvb)
    x = _merge(x, oa, ob, row(g_pre_mix), row(g_post_mix), wu, wg,
               w_br_a.astype(BF16), w_br_b.astype(BF16), w_br_c.astype(BF16), w_out.astype(BF16),
               w_pool.astype(BF16), row(pool_scale))
    return _mlp(x, row(g_pre_mlp), row(g_post_mlp), w_ff1.astype(BF16), w_ff2.astype(BF16))


@jax.jit
def kernel(x, w_in, w_br_a, w_br_b, w_br_c, w_out, w_pool, pool_scale, g_pre_mix, g_post_mix, g_pre_mlp, g_post_mlp, w_ff1, w_ff2):
    for l in range(w_in.shape[0]):
        x = _layer(x, w_in[l], w_br_a[l], w_br_b[l], w_br_c[l], w_out[l], w_pool[l], pool_scale[l],
                   g_pre_mix[l], g_post_mix[l], g_pre_mlp[l], g_post_mlp[l], w_ff1[l], w_ff2[l])
    return x
```

```python
import jax
import jax.numpy as jnp
import numpy as np
from jax import lax
from jax.experimental import pallas as pl
from jax.experimental.pallas import tpu as pltpu

F32 = jnp.float32
BF16 = jnp.bfloat16

D_MODEL = 1024
HEAD_DIM = 64
N_HEADS = 8
D_ATT = N_HEADS * HEAD_DIM
PAIR = 2 * HEAD_DIM
N_PAIRS = N_HEADS // 2
MOBA_BLOCK = 256
MOBA_TOPK = 3
MOBA_GROUP = 4
MOBA_QTILE = 512
DENOM_ROWS = 16
VT_ROWS = HEAD_DIM + DENOM_ROWS
LOG2_E = 1.4426950408889634
DIL_PATTERNS = ((128, 1), (512, 4), (2048, 16))
DIL_BAND = 128
DIL_TILE = 2048
DIL_SPLIT = 4
POOL_WINDOWS = (2, 4, 8, 16)
POOL_WIDTH = 512
POOL_GROUP_DIM = 128
POOL_HALO = 16
D_FF = 4 * D_MODEL
RMS_EPS = 1e-6
QK_SCALE = HEAD_DIM ** -0.5
NEG = -1e30

TOK_TILE = 1024
VMEM_LIMIT = 56 * 1024 * 1024

NT_DIMS = (((1,), (1,)), ((), ()))


def _dot(a, b):
    return jnp.dot(a, b, preferred_element_type=F32)


def _dot_nt(a, b):
    return lax.dot_general(a, b, NT_DIMS, preferred_element_type=F32)


def _split_bf16(v):
    hi = v.astype(BF16)
    lo = (v - hi.astype(F32)).astype(BF16)
    return hi, lo


def _rms(x, g):
    var = jnp.mean(x * x, axis=-1, keepdims=True)
    return x * lax.rsqrt(var + RMS_EPS) * g


def _in_proj_kernel(x_ref, g_ref, wtok_ref, wft_ref, ka_ref, qkvb_ref, qt_ref, bias_ref, vt_ref, kbar_ref):
    t = pl.program_id(1)
    blocks_per_tile = TOK_TILE // MOBA_BLOCK

    @pl.when(t == 0)
    def _():
        kbar_ref[...] = jnp.zeros_like(kbar_ref)

    h = _rms(x_ref[0], g_ref[...]).astype(BF16)

    tok = _dot(h, wtok_ref[...])
    ka = tok[:, 0:D_ATT]
    ka_ref[0] = ka.astype(BF16)
    for blk in range(blocks_per_tile):
        kbar_ref[pl.ds(t * blocks_per_tile + blk, 1), :] = jnp.mean(
            ka[blk * MOBA_BLOCK:(blk + 1) * MOBA_BLOCK], axis=0, keepdims=True)
    qkvb_ref[0, :, 0:D_ATT] = tok[:, D_ATT:2 * D_ATT] * QK_SCALE
    qkvb_ref[0, :, D_ATT:3 * D_ATT] = tok[:, 2 * D_ATT:4 * D_ATT]

    ft = _dot_nt(wft_ref[...], h)
    qt = ft[0:D_ATT] * (QK_SCALE * LOG2_E)
    qt_ref[0] = qt.astype(BF16)
    vt = ft[D_ATT:2 * D_ATT].astype(BF16)
    ones_rows = jnp.ones((DENOM_ROWS, TOK_TILE), BF16)
    for head in range(N_HEADS):
        vt_ref[0, head * VT_ROWS:head * VT_ROWS + HEAD_DIM, :] = vt[head * HEAD_DIM:(head + 1) * HEAD_DIM]
        vt_ref[0, head * VT_ROWS + HEAD_DIM:(head + 1) * VT_ROWS, :] = ones_rows

    half = D_ATT // 2
    r = lax.broadcasted_iota(jnp.int32, (half, half), 0) // HEAD_DIM
    c = lax.broadcasted_iota(jnp.int32, (half, half), 1) // HEAD_DIM
    same_head = r == c
    n_blocks = kbar_ref.shape[0]
    blk_row = lax.broadcasted_iota(jnp.int32, (n_blocks, TOK_TILE), 0).astype(F32)
    q_blk = (t * blocks_per_tile
             + lax.broadcasted_iota(jnp.int32, (n_blocks, TOK_TILE), 1) // MOBA_BLOCK).astype(F32)
    valid = blk_row < q_blk
    for quad in range(2):
        kq = kbar_ref[:, quad * half:(quad + 1) * half]
        kb = jnp.where(same_head, jnp.concatenate([kq] * 4, axis=0), 0.0)
        kb_hi, kb_lo = _split_bf16(kb)
        q_hi, q_lo = _split_bf16(qt[quad * half:(quad + 1) * half])
        gs = _dot(kb_hi, q_hi) + _dot(kb_hi, q_lo) + _dot(kb_lo, q_hi)
        for hq in range(4):
            g = jnp.where(valid, gs[hq * n_blocks:(hq + 1) * n_blocks], -jnp.inf)
            sel = jnp.zeros(g.shape, jnp.bool_)
            for _ in range(MOBA_TOPK):
                m = jnp.max(g, axis=0, keepdims=True)
                idx = jnp.min(jnp.where(g == m, blk_row, float(n_blocks)), axis=0, keepdims=True)
                pick = blk_row == idx
                sel = jnp.logical_or(sel, pick)
                g = jnp.where(pick, -jnp.inf, g)
            head = quad * 4 + hq
            bias_ref[0, head * n_blocks:(head + 1) * n_blocks, :] = jnp.where(
                jnp.logical_and(sel, valid), 0.0, NEG)


def _in_proj(x, g, wtok, wft):
    b, s, d = x.shape
    n_blocks = s // MOBA_BLOCK
    const = lambda shape: pl.BlockSpec(shape, lambda bi, ti: (0,) * len(shape), pipeline_mode=pl.Buffered(1))
    return pl.pallas_call(
        _in_proj_kernel,
        grid=(b, s // TOK_TILE),
        in_specs=[
            pl.BlockSpec((1, TOK_TILE, d), lambda bi, ti: (bi, ti, 0)),
            const((1, d)),
            const((d, 4 * D_ATT)), const((2 * D_ATT, d)),
        ],
        out_specs=[
            pl.BlockSpec((1, TOK_TILE, D_ATT), lambda bi, ti: (bi, ti, 0)),
            pl.BlockSpec((1, TOK_TILE, 3 * D_ATT), lambda bi, ti: (bi, ti, 0)),
            pl.BlockSpec((1, D_ATT, TOK_TILE), lambda bi, ti: (bi, 0, ti)),
            pl.BlockSpec((1, N_HEADS * n_blocks, TOK_TILE), lambda bi, ti: (bi, 0, ti)),
            pl.BlockSpec((1, N_HEADS * VT_ROWS, TOK_TILE), lambda bi, ti: (bi, 0, ti)),
        ],
        out_shape=[
            jax.ShapeDtypeStruct((b, s, D_ATT), BF16),
            jax.ShapeDtypeStruct((b, s, 3 * D_ATT), F32),
            jax.ShapeDtypeStruct((b, D_ATT, s), BF16),
            jax.ShapeDtypeStruct((b, N_HEADS * n_blocks, s), F32),
            jax.ShapeDtypeStruct((b, N_HEADS * VT_ROWS, s), BF16),
        ],
        scratch_shapes=[pltpu.VMEM((n_blocks, D_ATT), F32)],
        compiler_params=pltpu.CompilerParams(
            dimension_semantics=("arbitrary", "arbitrary"), vmem_limit_bytes=VMEM_LIMIT),
        name="in_proj",
    )(x, g, wtok, wft)


def _moba_steps(n_blocks):
    q_blocks = MOBA_QTILE // MOBA_BLOCK
    steps = [(0, 0, 1, 0)]
    for tile in range(1, n_blocks // q_blocks):
        n_past = tile * q_blocks - q_blocks
        steps.append((tile, n_past, 1, 0))
        for start in range(0, n_past, MOBA_GROUP):
            steps.append((tile, start, 0, 1 if start + MOBA_GROUP > n_past else 0))
    if len(steps) % 2 == 0:
        steps.append((steps[-1][0], steps[-1][1], 0, 2))
    return [np.asarray(col, np.int32) for col in zip(*steps)]


def _moba_kernel(tile_tab, start_tab, first_tab, off_tab, qt_ref, bias_ref, k_ref, vt_ref, o_ref, s_sc, p_sc, acc_sc):
    n_blocks = bias_ref.shape[1] // 2
    n_steps = tile_tab.shape[0]
    q_blocks = MOBA_QTILE // MOBA_BLOCK
    chunk_keys = MOBA_GROUP * MOBA_BLOCK
    future = (lax.broadcasted_iota(jnp.int32, (MOBA_BLOCK, MOBA_BLOCK), 0)
              > lax.broadcasted_iota(jnp.int32, (MOBA_BLOCK, MOBA_BLOCK), 1))
    qry_blk_row = lax.broadcasted_iota(jnp.int32, (1, MOBA_QTILE), 1) // MOBA_BLOCK
    zeros = jnp.zeros((HEAD_DIM, MOBA_QTILE), BF16)
    tail = tuple(range(MOBA_GROUP - q_blocks, MOBA_GROUP))

    def tile_lanes(n):
        return pl.ds(pl.multiple_of(tile_tab[n] * MOBA_QTILE, MOBA_QTILE), MOBA_QTILE)

    def chunk_keys_of(n):
        return pl.ds(pl.multiple_of(start_tab[n] * MOBA_BLOCK, q_blocks * MOBA_BLOCK), chunk_keys)

    def scores(n, slot, own_pos=tail):
        qt = qt_ref[0, :, tile_lanes(n)]
        qt_heads = (jnp.concatenate([qt[:HEAD_DIM], zeros], axis=0),
                    jnp.concatenate([zeros, qt[HEAD_DIM:]], axis=0))
        k_c = k_ref[0, chunk_keys_of(n), :]
        first = first_tab[n] == 1
        col_max = []
        for hd in range(2):
            s = _dot(k_c, qt_heads[hd])
            for r in range(MOBA_GROUP):
                s_r = s[r * MOBA_BLOCK:(r + 1) * MOBA_BLOCK]
                if r in own_pos:
                    cols = slice(own_pos.index(r) * MOBA_BLOCK, (own_pos.index(r) + 1) * MOBA_BLOCK)
                    parts = [s_r[:, :cols.start], s_r[:, cols] + jnp.where(jnp.logical_and(first, future), NEG, 0.0),
                             s_r[:, cols.stop:]]
                    s_r = jnp.concatenate([part for part in parts if part.shape[1]], axis=1)
                s_sc[slot, hd, r * MOBA_BLOCK:(r + 1) * MOBA_BLOCK, :] = s_r
                col_max.append(jnp.max(s_r, axis=0, keepdims=True))
        return tuple(col_max)

    def softmax(n, slot, m_pair, col_max, own_pos=tail):
        first = first_tab[n] == 1
        off = off_tab[n]
        alphas, m_news = [], []
        for hd in range(2):
            m_old = jnp.where(first, NEG, m_pair[hd])
            gates = []
            for r in range(MOBA_GROUP):
                gate = bias_ref[0, pl.ds(hd * n_blocks + start_tab[n] + r, 1), tile_lanes(n)]
                if r in own_pos:
                    gate = jnp.where(jnp.logical_and(first, qry_blk_row == own_pos.index(r)), 0.0, gate)
                gated_off = off >= 1 if r >= MOBA_GROUP - q_blocks else off >= 2
                gates.append(gate + jnp.where(gated_off, NEG, 0.0))
            m_new = m_old
            for r, gate in enumerate(gates):
                m_new = jnp.maximum(m_new, col_max[hd * MOBA_GROUP + r] + gate)
            alphas.append(jnp.exp2(m_old - m_new))
            m_news.append(m_new)
            for r, gate in enumerate(gates):
                rows = slice(r * MOBA_BLOCK, (r + 1) * MOBA_BLOCK)
                p_sc[slot, hd, rows, :] = jnp.exp2(s_sc[slot, hd, rows, :] - (m_new - gate)).astype(BF16)
        return tuple(alphas), tuple(m_news)

    def store(n, accs):
        ot = jnp.concatenate([a[:HEAD_DIM] * (1.0 / a[HEAD_DIM:HEAD_DIM + 1]) for a in accs], axis=0)
        o_ref[0, tile_lanes(n), :] = ot.T.astype(BF16)

    def accum(n, slot, alphas):
        accs = tuple(acc_sc[hd] for hd in range(2))
        store(jnp.maximum(n - 1, 0), accs)
        vt_c = vt_ref[0, :, chunk_keys_of(n)]
        for hd in range(2):
            acc_sc[hd] = alphas[hd] * accs[hd] + _dot(vt_c[hd * VT_ROWS:(hd + 1) * VT_ROWS], p_sc[slot, hd])

    m0 = (jnp.full((1, MOBA_QTILE), NEG, F32),) * 2
    acc_sc[...] = jnp.ones_like(acc_sc)
    head = tuple(range(q_blocks))
    alpha0, m1 = softmax(0, 0, m0, scores(0, 0, own_pos=head), own_pos=head)
    cm1 = scores(1, 1)

    def two_steps(t, carry):
        alpha_prev, m, cm_a = carry
        n_a = 2 * t + 1
        n_b = n_a + 1
        cm_b = scores(n_b, 0)
        alpha_a, m = softmax(n_a, 1, m, cm_a)
        accum(n_a - 1, 0, alpha_prev)
        cm_next = scores(jnp.minimum(n_b + 1, n_steps - 1), 1)
        alpha_b, m = softmax(n_b, 0, m, cm_b)
        accum(n_a, 1, alpha_a)
        return alpha_b, m, cm_next

    alpha_l, _, _ = lax.fori_loop(0, (n_steps - 1) // 2, two_steps, (alpha0, m1, cm1))
    accum(n_steps - 1, 0, alpha_l)
    store(n_steps - 1, tuple(acc_sc[hd] for hd in range(2)))


def _moba(qt, bias, ka, vt):
    b, s, _ = ka.shape
    n_blocks = s // MOBA_BLOCK
    tables = _moba_steps(n_blocks)
    whole = lambda shape, index: pl.BlockSpec(shape, index, pipeline_mode=pl.Buffered(1))
    return pl.pallas_call(
        _moba_kernel,
        grid_spec=pltpu.PrefetchScalarGridSpec(
            num_scalar_prefetch=len(tables),
            grid=(b, N_PAIRS),
            in_specs=[
                whole((1, PAIR, s), lambda bi, p, *_: (bi, p, 0)),
                whole((1, 2 * n_blocks, s), lambda bi, p, *_: (bi, p, 0)),
                whole((1, s, PAIR), lambda bi, p, *_: (bi, 0, p)),
                whole((1, 2 * VT_ROWS, s), lambda bi, p, *_: (bi, p, 0)),
            ],
            out_specs=pl.BlockSpec((1, s, PAIR), lambda bi, p, *_: (bi, 0, p)),
            scratch_shapes=[pltpu.VMEM((2, 2, MOBA_GROUP * MOBA_BLOCK, MOBA_QTILE), F32),
                            pltpu.VMEM((2, 2, MOBA_GROUP * MOBA_BLOCK, MOBA_QTILE), BF16),
                            pltpu.VMEM((2, VT_ROWS, MOBA_QTILE), F32)],
        ),
        out_shape=jax.ShapeDtypeStruct((b, s, D_ATT), BF16),
        compiler_params=pltpu.CompilerParams(
            dimension_semantics=("arbitrary", "arbitrary"), vmem_limit_bytes=VMEM_LIMIT),
        name="moba",
    )(*tables, qt, bias, ka, vt)


def _dilated_kernel(q_ref, kc_ref, kp_ref, vc_ref, vp_ref, o_ref, o_sc, lse_sc, split_sc):
    first_tile = pl.program_id(2) == 0
    n_tiles = DIL_TILE // DIL_BAND
    head_a = lax.broadcasted_iota(jnp.int32, (DIL_TILE, PAIR), 1) < HEAD_DIM
    qi = lax.broadcasted_iota(jnp.int32, (DIL_BAND, 2 * DIL_BAND), 0)
    kj = lax.broadcasted_iota(jnp.int32, (DIL_BAND, 2 * DIL_BAND), 1)
    band = jnp.where(jnp.logical_and(kj >= qi, kj <= qi + DIL_BAND), 0.0, NEG)
    band_start = jnp.where(first_tile, jnp.where(kj >= DIL_BAND, band, NEG), band)
    operands = (q_ref, kc_ref, kp_ref, vc_ref, vp_ref)

    def residue_rows(a, dil, r, first, count):
        if dil > DIL_SPLIT:
            sub = dil // DIL_SPLIT
            base = (r % DIL_SPLIT) * (DIL_TILE // DIL_SPLIT) + r // DIL_SPLIT
            return split_sc[a, pl.ds(base + sub * first, count, stride=sub), :]
        return operands[a][0, pl.ds(first * dil + r, count, stride=dil), :]

    for pat, (window, dil) in enumerate(DIL_PATTERNS):
        rows = DIL_TILE // dil
        n_q = rows // DIL_BAND
        halo = rows - DIL_BAND
        q_parts, k_tiles, v_tiles, masks = [], [], [], []
        for r in range(dil):
            if dil == DIL_SPLIT:
                q_r, kc_r, kp_r, vc_r, vp_r = whole = [residue_rows(a, dil, r, 0, rows) for a in range(5)]
                for a in range(5):
                    split_sc[a, r * rows:(r + 1) * rows, :] = whole[a]
                kp_r, vp_r = kp_r[halo:], vp_r[halo:]
            else:
                q_r, kc_r, vc_r = (residue_rows(a, dil, r, 0, rows) for a in (0, 1, 3))
                kp_r, vp_r = (residue_rows(a, dil, r, halo, DIL_BAND) for a in (2, 4))
            q_parts.append(q_r)
            k_ext = jnp.concatenate([kp_r, kc_r], axis=0).astype(BF16)
            v_ext = jnp.concatenate([vp_r, vc_r], axis=0).astype(BF16)
            for qb in range(n_q):
                k_tiles.append(k_ext[qb * DIL_BAND:(qb + 2) * DIL_BAND])
                v_tiles.append(v_ext[qb * DIL_BAND:(qb + 2) * DIL_BAND])
                masks.append(band_start if qb == 0 else band)
        q_all = jnp.concatenate(q_parts, axis=0)
        mask_all = jnp.concatenate(masks, axis=0)
        outs, lses = [], []
        for hd in range(2):
            qh = jnp.where(head_a if hd == 0 else jnp.logical_not(head_a), q_all, 0.0).astype(BF16)
            s = jnp.concatenate(
                [_dot_nt(qh[n * DIL_BAND:(n + 1) * DIL_BAND], k_tiles[n]) for n in range(n_tiles)],
                axis=0) + mask_all
            m = jnp.max(s, axis=-1, keepdims=True)
            p = jnp.exp(s - m)
            l = jnp.sum(p, axis=-1, keepdims=True)
            pb = p.astype(BF16)
            o = jnp.concatenate(
                [_dot(pb[n * DIL_BAND:(n + 1) * DIL_BAND], v_tiles[n]) for n in range(n_tiles)], axis=0)
            outs.append(o / l)
            lses.append(m + jnp.log(l))
        o_pair = jnp.where(head_a, outs[0], outs[1])
        lse_pair = jnp.where(head_a, lses[0], lses[1])
        for r in range(dil):
            o_sc[pat, pl.ds(r, rows, stride=dil), :] = o_pair[r * rows:(r + 1) * rows]
            lse_sc[pat, pl.ds(r, rows, stride=dil), :] = lse_pair[r * rows:(r + 1) * rows]

    lse = lse_sc[...]
    w = jnp.exp(lse - jnp.max(lse, axis=0, keepdims=True))
    o_ref[0] = (jnp.sum(w * o_sc[...], axis=0) / jnp.sum(w, axis=0)).astype(BF16)


def _dilated(qkvb):
    b, s, _ = qkvb.shape
    tile_spec = lambda col0, prev: pl.BlockSpec(
        (1, DIL_TILE, PAIR),
        (lambda bi, p, t: (bi, jnp.maximum(t - 1, 0), col0 + p)) if prev else (lambda bi, p, t: (bi, t, col0 + p)))
    return pl.pallas_call(
        _dilated_kernel,
        grid=(b, N_PAIRS, s // DIL_TILE),
        in_specs=[tile_spec(0, False),
                  tile_spec(N_PAIRS, False), tile_spec(N_PAIRS, True),
                  tile_spec(2 * N_PAIRS, False), tile_spec(2 * N_PAIRS, True)],
        out_specs=pl.BlockSpec((1, DIL_TILE, PAIR), lambda bi, p, t: (bi, t, p)),
        out_shape=jax.ShapeDtypeStruct((b, s, D_ATT), BF16),
        scratch_shapes=[pltpu.VMEM((len(DIL_PATTERNS), DIL_TILE, PAIR), F32),
                        pltpu.VMEM((len(DIL_PATTERNS), DIL_TILE, PAIR), F32),
                        pltpu.VMEM((5, DIL_TILE, PAIR), F32)],
        compiler_params=pltpu.CompilerParams(
            dimension_semantics=("arbitrary", "arbitrary", "arbitrary"), vmem_limit_bytes=VMEM_LIMIT),
        name="dilated",
    )(qkvb, qkvb, qkvb, qkvb, qkvb)


def _merge_kernel(x_ref, xh_ref, oa_ref, ob_ref, gpre_ref, gpost_ref, wu_ref, wg_ref, wa_ref, wb_ref, wc_ref,
                  wout_ref, wpool_ref, pscale_ref, o_ref):
    t = pl.program_id(1)
    x = x_ref[0]
    h = _rms(x, gpre_ref[...]).astype(BF16)
    hh = _rms(xh_ref[0], gpre_ref[...]).astype(BF16)
    u = _dot(h, wu_ref[...])
    uh = jnp.where(t == 0, 0.0, _dot(hh, wu_ref[...]))
    u_ext = jnp.concatenate([uh, u], axis=0)
    pos = (t * TOK_TILE + lax.broadcasted_iota(jnp.int32, (TOK_TILE, POOL_GROUP_DIM), 0) + 1).astype(F32)
    oc = []
    for g, w in enumerate(POOL_WINDOWS):
        cols = slice(g * POOL_GROUP_DIM, (g + 1) * POOL_GROUP_DIM)
        acc = u_ext[:, cols]
        span = 1
        while span < w:
            acc = acc + pltpu.roll(acc, span, 0)
            span *= 2
        pooled = acc[POOL_HALO:] / jnp.minimum(pos, float(w)) - u[:, cols]
        oc.append(_dot(pooled.astype(BF16), wpool_ref[g]))
    oc = (jnp.concatenate(oc, axis=1) * pscale_ref[...]).astype(BF16)

    merged = None
    for br, (src, w_ref) in enumerate(((oa_ref[0], wa_ref), (ob_ref[0], wb_ref), (oc, wc_ref))):
        gz = _dot(h, wg_ref[:, br * D_MODEL:(br + 1) * D_MODEL])
        term = (1.0 / (1.0 + jnp.exp(-gz))) * _dot(src, w_ref[...])
        merged = term if merged is None else merged + term
    y = _dot(merged.astype(BF16), wout_ref[...])
    o_ref[0] = x + _rms(y, gpost_ref[...])


def _merge(x, oa, ob, gpre, gpost, wu, wg, wa, wb, wc, wout, wpool, pscale):
    b, s, d = x.shape
    const = lambda shape: pl.BlockSpec(shape, lambda bi, ti: (0,) * len(shape), pipeline_mode=pl.Buffered(1))
    halo_blocks = TOK_TILE // POOL_HALO
    return pl.pallas_call(
        _merge_kernel,
        grid=(b, s // TOK_TILE),
        in_specs=[
            pl.BlockSpec((1, TOK_TILE, d), lambda bi, ti: (bi, ti, 0)),
            pl.BlockSpec((1, POOL_HALO, d), lambda bi, ti: (bi, jnp.maximum(ti * halo_blocks - 1, 0), 0)),
            pl.BlockSpec((1, TOK_TILE, D_ATT), lambda bi, ti: (bi, ti, 0)),
            pl.BlockSpec((1, TOK_TILE, D_ATT), lambda bi, ti: (bi, ti, 0)),
            const((1, d)), const((1, d)),
            const((d, POOL_WIDTH)), const((d, 3 * d)),
            const((D_ATT, d)), const((D_ATT, d)), const((POOL_WIDTH, d)),
            const((d, d)), const((len(POOL_WINDOWS), POOL_GROUP_DIM, POOL_GROUP_DIM)), const((1, POOL_WIDTH)),
        ],
        out_specs=pl.BlockSpec((1, TOK_TILE, d), lambda bi, ti: (bi, ti, 0)),
        out_shape=jax.ShapeDtypeStruct((b, s, d), F32),
        compiler_params=pltpu.CompilerParams(
            dimension_semantics=("arbitrary", "arbitrary"), vmem_limit_bytes=VMEM_LIMIT),
        name="merge",
    )(x, x, oa, ob, gpre, gpost, wu, wg, wa, wb, wc, wout, wpool, pscale)


def _mlp_kernel(x_ref, gpre_ref, gpost_ref, w1_ref, w2_ref, o_ref):
    x = x_ref[0]
    h = _rms(x, gpre_ref[...]).astype(BF16)
    y = None
    for c in range(D_FF // D_MODEL):
        cols = slice(c * D_MODEL, (c + 1) * D_MODEL)
        a = jnp.maximum(_dot(h, w1_ref[:, cols]), 0.0)
        part = _dot((a * a).astype(BF16), w2_ref[cols, :])
        y = part if y is None else y + part
    o_ref[0] = x + _rms(y, gpost_ref[...])


def _mlp(x, gpre, gpost, w1, w2):
    b, s, d = x.shape
    const = lambda shape: pl.BlockSpec(shape, lambda bi, ti: (0,) * len(shape), pipeline_mode=pl.Buffered(1))
    return pl.pallas_call(
        _mlp_kernel,
        grid=(b, s // TOK_TILE),
        in_specs=[pl.BlockSpec((1, TOK_TILE, d), lambda bi, ti: (bi, ti, 0)),
                  const((1, d)), const((1, d)), const((d, D_FF)), const((D_FF, d))],
        out_specs=pl.BlockSpec((1, TOK_TILE, d), lambda bi, ti: (bi, ti, 0)),
        out_shape=jax.ShapeDtypeStruct((b, s, d), F32),
        compiler_params=pltpu.CompilerParams(
            dimension_semantics=("arbitrary", "arbitrary"), vmem_limit_bytes=VMEM_LIMIT),
        name="mlp",
    )(x, gpre, gpost, w1, w2)


def _layer(x, w_in, w_br_a, w_br_b, w_br_c, w_out, w_pool, pool_scale,
           g_pre_mix, g_post_mix, g_pre_mlp, g_post_mlp, w_ff1, w_ff2):
    col = lambda k: slice(k * D_ATT, (k + 1) * D_ATT)
    wtok = w_in[:, D_ATT:6 * D_ATT]
    wtok = jnp.concatenate([wtok[:, :D_ATT], wtok[:, 2 * D_ATT:]], axis=1).astype(BF16)
    wft = jnp.concatenate([w_in[:, col(0)], w_in[:, col(2)]], axis=1).T.astype(BF16)
    wu = w_in[:, 6 * D_ATT:6 * D_ATT + POOL_WIDTH].astype(BF16)
    wg = w_in[:, 6 * D_ATT + POOL_WIDTH:].astype(BF16)
    row = lambda v: v.reshape(1, -1)

    ka, qkvb, qt, bias, vt = _in_proj(x, row(g_pre_mix), wtok, wft)
    oa = _moba(qt, bias, ka, vt)
    ob = _dilated(qkvb)
    x = _merge(x, oa, ob, row(g_pre_mix), row(g_post_mix), wu, wg,
               w_br_a.astype(BF16), w_br_b.astype(BF16), w_br_c.astype(BF16), w_out.astype(BF16),
               w_pool.astype(BF16), row(pool_scale))
    return _mlp(x, row(g_pre_mlp), row(g_post_mlp), w_ff1.astype(BF16), w_ff2.astype(BF16))


@jax.jit
def kernel(x, w_in, w_br_a, w_br_b, w_br_c, w_out, w_pool, pool_scale, g_pre_mix, g_post_mix, g_pre_mlp, g_post_mlp, w_ff1, w_ff2):
    for l in range(w_in.shape[0]):
        x = _layer(x, w_in[l], w_br_a[l], w_br_b[l], w_br_c[l], w_out[l], w_pool[l], pool_scale[l],
                   g_pre_mix[l], g_post_mix[l], g_pre_mlp[l], g_post_mlp[l], w_ff1[l], w_ff2[l])
    return x
```

```python
import jax
import jax.numpy as jnp
import numpy as np
from jax import lax
from jax.experimental import pallas as pl
from jax.experimental.pallas import tpu as pltpu

F32 = jnp.float32
BF16 = jnp.bfloat16

D_MODEL = 1024
HEAD_DIM = 64
N_HEADS = 8
D_ATT = N_HEADS * HEAD_DIM
PAIR = 2 * HEAD_DIM
N_PAIRS = N_HEADS // 2
MOBA_BLOCK = 256
MOBA_TOPK = 3
MOBA_GROUP = 4
MOBA_QTILE = 512
DENOM_ROWS = 16
VT_ROWS = HEAD_DIM + DENOM_ROWS
LOG2_E = 1.4426950408889634
DIL_PATTERNS = ((128, 1), (512, 4), (2048, 16))
DIL_BAND = 128
DIL_TILE = 2048
DIL_SPLIT = 4
POOL_WINDOWS = (2, 4, 8, 16)
POOL_WIDTH = 512
POOL_GROUP_DIM = 128
POOL_HALO = 16
D_FF = 4 * D_MODEL
RMS_EPS = 1e-6
QK_SCALE = HEAD_DIM ** -0.5
NEG = -1e30

TOK_TILE = 1024
VMEM_LIMIT = 56 * 1024 * 1024

NT_DIMS = (((1,), (1,)), ((), ()))


def _dot(a, b):
    return jnp.dot(a, b, preferred_element_type=F32)


def _dot_nt(a, b):
    return lax.dot_general(a, b, NT_DIMS, preferred_element_type=F32)


def _split_bf16(v):
    hi = v.astype(BF16)
    lo = (v - hi.astype(F32)).astype(BF16)
    return hi, lo


def _rms(x, g):
    var = jnp.mean(x * x, axis=-1, keepdims=True)
    return x * lax.rsqrt(var + RMS_EPS) * g


def _in_proj_kernel(x_ref, g_ref, wtok_ref, wft_ref, ka_ref, qkvb_ref, qt_ref, bias_ref, vt_ref, kbar_ref):
    t = pl.program_id(1)
    blocks_per_tile = TOK_TILE // MOBA_BLOCK

    @pl.when(t == 0)
    def _():
        kbar_ref[...] = jnp.zeros_like(kbar_ref)

    h = _rms(x_ref[0], g_ref[...]).astype(BF16)

    tok = _dot(h, wtok_ref[...])
    ka = tok[:, 0:D_ATT]
    ka_ref[0] = ka.astype(BF16)
    for blk in range(blocks_per_tile):
        kbar_ref[pl.ds(t * blocks_per_tile + blk, 1), :] = jnp.mean(
            ka[blk * MOBA_BLOCK:(blk + 1) * MOBA_BLOCK], axis=0, keepdims=True)
    qkvb_ref[0, :, 0:D_ATT] = tok[:, D_ATT:2 * D_ATT] * QK_SCALE
    qkvb_ref[0, :, D_ATT:3 * D_ATT] = tok[:, 2 * D_ATT:4 * D_ATT]

    ft = _dot_nt(wft_ref[...], h)
    qt = ft[0:D_ATT] * (QK_SCALE * LOG2_E)
    qt_ref[0] = qt.astype(BF16)
    vt = ft[D_ATT:2 * D_ATT].astype(BF16)
    ones_rows = jnp.ones((DENOM_ROWS, TOK_TILE), BF16)
    for head in range(N_HEADS):
        vt_ref[0, head * VT_ROWS:head * VT_ROWS + HEAD_DIM, :] = vt[head * HEAD_DIM:(head + 1) * HEAD_DIM]
        vt_ref[0, head * VT_ROWS + HEAD_DIM:(head + 1) * VT_ROWS, :] = ones_rows

    half = D_ATT // 2
    r = lax.broadcasted_iota(jnp.int32, (half, half), 0) // HEAD_DIM
    c = lax.broadcasted_iota(jnp.int32, (half, half), 1) // HEAD_DIM
    same_head = r == c
    n_blocks = kbar_ref.shape[0]
    blk_row = lax.broadcasted_iota(jnp.int32, (n_blocks, TOK_TILE), 0).astype(F32)
    q_blk = (t * blocks_per_tile
             + lax.broadcasted_iota(jnp.int32, (n_blocks, TOK_TILE), 1) // MOBA_BLOCK).astype(F32)
    valid = blk_row < q_blk
    for quad in range(2):
        kq = kbar_ref[:, quad * half:(quad + 1) * half]
        kb = jnp.where(same_head, jnp.concatenate([kq] * 4, axis=0), 0.0)
        kb_hi, kb_lo = _split_bf16(kb)
        q_hi, q_lo = _split_bf16(qt[quad * half:(quad + 1) * half])
        gs = _dot(kb_hi, q_hi) + _dot(kb_hi, q_lo) + _dot(kb_lo, q_hi)
        for hq in range(4):
            g = jnp.where(valid, gs[hq * n_blocks:(hq + 1) * n_blocks], -jnp.inf)
            sel = jnp.zeros(g.shape, jnp.bool_)
            for _ in range(MOBA_TOPK):
                m = jnp.max(g, axis=0, keepdims=True)
                idx = jnp.min(jnp.where(g == m, blk_row, float(n_blocks)), axis=0, keepdims=True)
                pick = blk_row == idx
                sel = jnp.logical_or(sel, pick)
                g = jnp.where(pick, -jnp.inf, g)
            head = quad * 4 + hq
            bias_ref[0, head * n_blocks:(head + 1) * n_blocks, :] = jnp.where(
                jnp.logical_and(sel, valid), 0.0, NEG)


def _in_proj(x, g, wtok, wft):
    b, s, d = x.shape
    n_blocks = s // MOBA_BLOCK
    const = lambda shape: pl.BlockSpec(shape, lambda bi, ti: (0,) * len(shape), pipeline_mode=pl.Buffered(1))
    return pl.pallas_call(
        _in_proj_kernel,
        grid=(b, s // TOK_TILE),
        in_specs=[
            pl.BlockSpec((1, TOK_TILE, d), lambda bi, ti: (bi, ti, 0)),
            const((1, d)),
            const((d, 4 * D_ATT)), const((2 * D_ATT, d)),
        ],
        out_specs=[
            pl.BlockSpec((1, TOK_TILE, D_ATT), lambda bi, ti: (bi, ti, 0)),
            pl.BlockSpec((1, TOK_TILE, 3 * D_ATT), lambda bi, ti: (bi, ti, 0)),
            pl.BlockSpec((1, D_ATT, TOK_TILE), lambda bi, ti: (bi, 0, ti)),
            pl.BlockSpec((1, N_HEADS * n_blocks, TOK_TILE), lambda bi, ti: (bi, 0, ti)),
            pl.BlockSpec((1, N_HEADS * VT_ROWS, TOK_TILE), lambda bi, ti: (bi, 0, ti)),
        ],
        out_shape=[
            jax.ShapeDtypeStruct((b, s, D_ATT), BF16),
            jax.ShapeDtypeStruct((b, s, 3 * D_ATT), F32),
            jax.ShapeDtypeStruct((b, D_ATT, s), BF16),
            jax.ShapeDtypeStruct((b, N_HEADS * n_blocks, s), F32),
            jax.ShapeDtypeStruct((b, N_HEADS * VT_ROWS, s), BF16),
        ],
        scratch_shapes=[pltpu.VMEM((n_blocks, D_ATT), F32)],
        compiler_params=pltpu.CompilerParams(
            dimension_semantics=("arbitrary", "arbitrary"), vmem_limit_bytes=VMEM_LIMIT),
        name="in_proj",
    )(x, g, wtok, wft)


def _moba_steps(n_blocks):
    q_blocks = MOBA_QTILE // MOBA_BLOCK
    steps = [(0, 0, 1, 0)]
    for tile in range(1, n_blocks // q_blocks):
        n_past = tile * q_blocks - q_blocks
        steps.append((tile, n_past, 1, 0))
        for start in range(0, n_past, MOBA_GROUP):
            steps.append((tile, start, 0, 1 if start + MOBA_GROUP > n_past else 0))
    if len(steps) % 2 == 0:
        steps.append((steps[-1][0], steps[-1][1], 0, 2))
    return [np.asarray(col, np.int32) for col in zip(*steps)]


def _moba_kernel(tile_tab, start_tab, first_tab, off_tab, qt_ref, bias_ref, k_ref, vt_ref, o_ref, s_sc, p_sc, acc_sc, cm_sc):
    n_blocks = bias_ref.shape[1] // 2
    n_steps = tile_tab.shape[0]
    q_blocks = MOBA_QTILE // MOBA_BLOCK
    chunk_keys = MOBA_GROUP * MOBA_BLOCK
    future = (lax.broadcasted_iota(jnp.int32, (MOBA_BLOCK, MOBA_BLOCK), 0)
              > lax.broadcasted_iota(jnp.int32, (MOBA_BLOCK, MOBA_BLOCK), 1))
    qry_blk_row = lax.broadcasted_iota(jnp.int32, (1, MOBA_QTILE), 1) // MOBA_BLOCK
    zeros = jnp.zeros((HEAD_DIM, MOBA_QTILE), BF16)
    tail = tuple(range(MOBA_GROUP - q_blocks, MOBA_GROUP))

    def tile_lanes(n):
        return pl.ds(pl.multiple_of(tile_tab[n] * MOBA_QTILE, MOBA_QTILE), MOBA_QTILE)

    def chunk_keys_of(n):
        return pl.ds(pl.multiple_of(start_tab[n] * MOBA_BLOCK, q_blocks * MOBA_BLOCK), chunk_keys)

    def scores(n, slot, own_pos=tail):
        qt = qt_ref[0, :, tile_lanes(n)]
        qt_heads = (jnp.concatenate([qt[:HEAD_DIM], zeros], axis=0),
                    jnp.concatenate([zeros, qt[HEAD_DIM:]], axis=0))
        k_c = k_ref[0, chunk_keys_of(n), :]
        first = first_tab[n] == 1
        for hd in range(2):
            s = _dot(k_c, qt_heads[hd])
            for r in range(MOBA_GROUP):
                s_r = s[r * MOBA_BLOCK:(r + 1) * MOBA_BLOCK]
                if r in own_pos:
                    cols = slice(own_pos.index(r) * MOBA_BLOCK, (own_pos.index(r) + 1) * MOBA_BLOCK)
                    parts = [s_r[:, :cols.start], s_r[:, cols] + jnp.where(jnp.logical_and(first, future), NEG, 0.0),
                             s_r[:, cols.stop:]]
                    s_r = jnp.concatenate([part for part in parts if part.shape[1]], axis=1)
                s_sc[slot, hd, r * MOBA_BLOCK:(r + 1) * MOBA_BLOCK, :] = s_r
                cm_sc[slot, hd * MOBA_GROUP + r:hd * MOBA_GROUP + r + 1, :] = jnp.max(s_r, axis=0, keepdims=True)

    def softmax(n, slot, m_pair, own_pos=tail):
        first = first_tab[n] == 1
        off = off_tab[n]
        alphas, m_news = [], []
        for hd in range(2):
            m_old = jnp.where(first, NEG, m_pair[hd])
            gates = []
            for r in range(MOBA_GROUP):
                gate = bias_ref[0, pl.ds(hd * n_blocks + start_tab[n] + r, 1), tile_lanes(n)]
                if r in own_pos:
                    gate = jnp.where(jnp.logical_and(first, qry_blk_row == own_pos.index(r)), 0.0, gate)
                gated_off = off >= 1 if r >= MOBA_GROUP - q_blocks else off >= 2
                gates.append(gate + jnp.where(gated_off, NEG, 0.0))
            m_new = m_old
            for r, gate in enumerate(gates):
                m_new = jnp.maximum(m_new, cm_sc[slot, hd * MOBA_GROUP + r:hd * MOBA_GROUP + r + 1, :] + gate)
            alphas.append(jnp.exp2(m_old - m_new))
            m_news.append(m_new)
            for r, gate in enumerate(gates):
                rows = slice(r * MOBA_BLOCK, (r + 1) * MOBA_BLOCK)
                p_sc[slot, hd, rows, :] = jnp.exp2(s_sc[slot, hd, rows, :] - (m_new - gate)).astype(BF16)
        return tuple(alphas), tuple(m_news)

    def store(n, accs):
        ot = jnp.concatenate([a[:HEAD_DIM] * (1.0 / a[HEAD_DIM:HEAD_DIM + 1]) for a in accs], axis=0)
        o_ref[0, tile_lanes(n), :] = ot.T.astype(BF16)

    def accum(n, slot, alphas):
        accs = tuple(acc_sc[hd] for hd in range(2))
        store(jnp.maximum(n - 1, 0), accs)
        vt_c = vt_ref[0, :, chunk_keys_of(n)]
        for hd in range(2):
            acc_sc[hd] = alphas[hd] * accs[hd] + _dot(vt_c[hd * VT_ROWS:(hd + 1) * VT_ROWS], p_sc[slot, hd])

    m0 = (jnp.full((1, MOBA_QTILE), NEG, F32),) * 2
    acc_sc[...] = jnp.ones_like(acc_sc)
    head = tuple(range(q_blocks))
    scores(0, 0, own_pos=head)
    alpha0, m1 = softmax(0, 0, m0, own_pos=head)
    scores(1, 1)

    def two_steps(t, carry):
        alpha_prev, m = carry
        n_a = 2 * t + 1
        n_b = n_a + 1
        scores(n_b, 0)
        alpha_a, m = softmax(n_a, 1, m)
        accum(n_a - 1, 0, alpha_prev)
        scores(jnp.minimum(n_b + 1, n_steps - 1), 1)
        alpha_b, m = softmax(n_b, 0, m)
        accum(n_a, 1, alpha_a)
        return alpha_b, m

    alpha_l, _ = lax.fori_loop(0, (n_steps - 1) // 2, two_steps, (alpha0, m1))
    accum(n_steps - 1, 0, alpha_l)
    store(n_steps - 1, tuple(acc_sc[hd] for hd in range(2)))


def _moba(qt, bias, ka, vt):
    b, s, _ = ka.shape
    n_blocks = s // MOBA_BLOCK
    tables = _moba_steps(n_blocks)
    whole = lambda shape, index: pl.BlockSpec(shape, index, pipeline_mode=pl.Buffered(1))
    return pl.pallas_call(
        _moba_kernel,
        grid_spec=pltpu.PrefetchScalarGridSpec(
            num_scalar_prefetch=len(tables),
            grid=(b, N_PAIRS),
            in_specs=[
                whole((1, PAIR, s), lambda bi, p, *_: (bi, p, 0)),
                whole((1, 2 * n_blocks, s), lambda bi, p, *_: (bi, p, 0)),
                whole((1, s, PAIR), lambda bi, p, *_: (bi, 0, p)),
                whole((1, 2 * VT_ROWS, s), lambda bi, p, *_: (bi, p, 0)),
            ],
            out_specs=pl.BlockSpec((1, s, PAIR), lambda bi, p, *_: (bi, 0, p)),
            scratch_shapes=[pltpu.VMEM((2, 2, MOBA_GROUP * MOBA_BLOCK, MOBA_QTILE), F32),
                            pltpu.VMEM((2, 2, MOBA_GROUP * MOBA_BLOCK, MOBA_QTILE), BF16),
                            pltpu.VMEM((2, VT_ROWS, MOBA_QTILE), F32),
                            pltpu.VMEM((2, 2 * MOBA_GROUP, MOBA_QTILE), F32)],
        ),
        out_shape=jax.ShapeDtypeStruct((b, s, D_ATT), BF16),
        compiler_params=pltpu.CompilerParams(
            dimension_semantics=("arbitrary", "arbitrary"), vmem_limit_bytes=VMEM_LIMIT),
        name="moba",
    )(*tables, qt, bias, ka, vt)


def _dilated_kernel(q_ref, kc_ref, kp_ref, vc_ref, vp_ref, o_ref, o_sc, lse_sc, split_sc):
    first_tile = pl.program_id(2) == 0
    n_tiles = DIL_TILE // DIL_BAND
    head_a = lax.broadcasted_iota(jnp.int32, (DIL_TILE, PAIR), 1) < HEAD_DIM
    qi = lax.broadcasted_iota(jnp.int32, (DIL_BAND, 2 * DIL_BAND), 0)
    kj = lax.broadcasted_iota(jnp.int32, (DIL_BAND, 2 * DIL_BAND), 1)
    band = jnp.where(jnp.logical_and(kj >= qi, kj <= qi + DIL_BAND), 0.0, NEG)
    band_start = jnp.where(first_tile, jnp.where(kj >= DIL_BAND, band, NEG), band)
    operands = (q_ref, kc_ref, kp_ref, vc_ref, vp_ref)

    def residue_rows(a, dil, r, first, count):
        if dil > DIL_SPLIT:
            sub = dil // DIL_SPLIT
            base = (r % DIL_SPLIT) * (DIL_TILE // DIL_SPLIT) + r // DIL_SPLIT
            return split_sc[a, pl.ds(base + sub * first, count, stride=sub), :]
        return operands[a][0, pl.ds(first * dil + r, count, stride=dil), :]

    for pat, (window, dil) in enumerate(DIL_PATTERNS):
        rows = DIL_TILE // dil
        n_q = rows // DIL_BAND
        halo = rows - DIL_BAND
        q_parts, k_tiles, v_tiles, masks = [], [], [], []
        for r in range(dil):
            if dil == DIL_SPLIT:
                q_r, kc_r, kp_r, vc_r, vp_r = whole = [residue_rows(a, dil, r, 0, rows) for a in range(5)]
                for a in range(5):
                    split_sc[a, r * rows:(r + 1) * rows, :] = whole[a]
                kp_r, vp_r = kp_r[halo:], vp_r[halo:]
            else:
                q_r, kc_r, vc_r = (residue_rows(a, dil, r, 0, rows) for a in (0, 1, 3))
                kp_r, vp_r = (residue_rows(a, dil, r, halo, DIL_BAND) for a in (2, 4))
            q_parts.append(q_r)
            k_ext = jnp.concatenate([kp_r, kc_r], axis=0).astype(BF16)
            v_ext = jnp.concatenate([vp_r, vc_r], axis=0).astype(BF16)
            for qb in range(n_q):
                k_tiles.append(k_ext[qb * DIL_BAND:(qb + 2) * DIL_BAND])
                v_tiles.append(v_ext[qb * DIL_BAND:(qb + 2) * DIL_BAND])
                masks.append(band_start if qb == 0 else band)
        q_all = jnp.concatenate(q_parts, axis=0)
        mask_all = jnp.concatenate(masks, axis=0)
        outs, lses = [], []
        for hd in range(2):
            qh = jnp.where(head_a if hd == 0 else jnp.logical_not(head_a), q_all, 0.0).astype(BF16)
            s = jnp.concatenate(
                [_dot_nt(qh[n * DIL_BAND:(n + 1) * DIL_BAND], k_tiles[n]) for n in range(n_tiles)],
                axis=0) + mask_all
            m = jnp.max(s, axis=-1, keepdims=True)
            p = jnp.exp(s - m)
            l = jnp.sum(p, axis=-1, keepdims=True)
            pb = p.astype(BF16)
            o = jnp.concatenate(
                [_dot(pb[n * DIL_BAND:(n + 1) * DIL_BAND], v_tiles[n]) for n in range(n_tiles)], axis=0)
            outs.append(o / l)
            lses.append(m + jnp.log(l))
        o_pair = jnp.where(head_a, outs[0], outs[1])
        lse_pair = jnp.where(head_a, lses[0], lses[1])
        for r in range(dil):
            o_sc[pat, pl.ds(r, rows, stride=dil), :] = o_pair[r * rows:(r + 1) * rows]
            lse_sc[pat, pl.ds(r, rows, stride=dil), :] = lse_pair[r * rows:(r + 1) * rows]

    lse = lse_sc[...]
    w = jnp.exp(lse - jnp.max(lse, axis=0, keepdims=True))
    o_ref[0] = (jnp.sum(w * o_sc[...], axis=0) / jnp.sum(w, axis=0)).astype(BF16)


def _dilated(qkvb):
    b, s, _ = qkvb.shape
    tile_spec = lambda col0, prev: pl.BlockSpec(
        (1, DIL_TILE, PAIR),
        (lambda bi, p, t: (bi, jnp.maximum(t - 1, 0), col0 + p)) if prev else (lambda bi, p, t: (bi, t, col0 + p)))
    return pl.pallas_call(
        _dilated_kernel,
        grid=(b, N_PAIRS, s // DIL_TILE),
        in_specs=[tile_spec(0, False),
                  tile_spec(N_PAIRS, False), tile_spec(N_PAIRS, True),
                  tile_spec(2 * N_PAIRS, False), tile_spec(2 * N_PAIRS, True)],
        out_specs=pl.BlockSpec((1, DIL_TILE, PAIR), lambda bi, p, t: (bi, t, p)),
        out_shape=jax.ShapeDtypeStruct((b, s, D_ATT), BF16),
        scratch_shapes=[pltpu.VMEM((len(DIL_PATTERNS), DIL_TILE, PAIR), F32),
                        pltpu.VMEM((len(DIL_PATTERNS), DIL_TILE, PAIR), F32),
                        pltpu.VMEM((5, DIL_TILE, PAIR), F32)],
        compiler_params=pltpu.CompilerParams(
            dimension_semantics=("arbitrary", "arbitrary", "arbitrary"), vmem_limit_bytes=VMEM_LIMIT),
        name="dilated",
    )(qkvb, qkvb, qkvb, qkvb, qkvb)


def _merge_kernel(x_ref, xh_ref, oa_ref, ob_ref, gpre_ref, gpost_ref, wu_ref, wg_ref, wa_ref, wb_ref, wc_ref,
                  wout_ref, wpool_ref, pscale_ref, o_ref):
    t = pl.program_id(1)
    x = x_ref[0]
    h = _rms(x, gpre_ref[...]).astype(BF16)
    hh = _rms(xh_ref[0], gpre_ref[...]).astype(BF16)
    u = _dot(h, wu_ref[...])
    uh = jnp.where(t == 0, 0.0, _dot(hh, wu_ref[...]))
    u_ext = jnp.concatenate([uh, u], axis=0)
    pos = (t * TOK_TILE + lax.broadcasted_iota(jnp.int32, (TOK_TILE, POOL_GROUP_DIM), 0) + 1).astype(F32)
    oc = []
    for g, w in enumerate(POOL_WINDOWS):
        cols = slice(g * POOL_GROUP_DIM, (g + 1) * POOL_GROUP_DIM)
        acc = u_ext[:, cols]
        span = 1
        while span < w:
            acc = acc + pltpu.roll(acc, span, 0)
            span *= 2
        pooled = acc[POOL_HALO:] / jnp.minimum(pos, float(w)) - u[:, cols]
        oc.append(_dot(pooled.astype(BF16), wpool_ref[g]))
    oc = (jnp.concatenate(oc, axis=1) * pscale_ref[...]).astype(BF16)

    merged = None
    for br, (src, w_ref) in enumerate(((oa_ref[0], wa_ref), (ob_ref[0], wb_ref), (oc, wc_ref))):
        gz = _dot(h, wg_ref[:, br * D_MODEL:(br + 1) * D_MODEL])
        term = (1.0 / (1.0 + jnp.exp(-gz))) * _dot(src, w_ref[...])
        merged = term if merged is None else merged + term
    y = _dot(merged.astype(BF16), wout_ref[...])
    o_ref[0] = x + _rms(y, gpost_ref[...])


def _merge(x, oa, ob, gpre, gpost, wu, wg, wa, wb, wc, wout, wpool, pscale):
    b, s, d = x.shape
    const = lambda shape: pl.BlockSpec(shape, lambda bi, ti: (0,) * len(shape), pipeline_mode=pl.Buffered(1))
    halo_blocks = TOK_TILE // POOL_HALO
    return pl.pallas_call(
        _merge_kernel,
        grid=(b, s // TOK_TILE),
        in_specs=[
            pl.BlockSpec((1, TOK_TILE, d), lambda bi, ti: (bi, ti, 0)),
            pl.BlockSpec((1, POOL_HALO, d), lambda bi, ti: (bi, jnp.maximum(ti * halo_blocks - 1, 0), 0)),
            pl.BlockSpec((1, TOK_TILE, D_ATT), lambda bi, ti: (bi, ti, 0)),
            pl.BlockSpec((1, TOK_TILE, D_ATT), lambda bi, ti: (bi, ti, 0)),
            const((1, d)), const((1, d)),
            const((d, POOL_WIDTH)), const((d, 3 * d)),
            const((D_ATT, d)), const((D_ATT, d)), const((POOL_WIDTH, d)),
            const((d, d)), const((len(POOL_WINDOWS), POOL_GROUP_DIM, POOL_GROUP_DIM)), const((1, POOL_WIDTH)),
        ],
        out_specs=pl.BlockSpec((1, TOK_TILE, d), lambda bi, ti: (bi, ti, 0)),
        out_shape=jax.ShapeDtypeStruct((b, s, d), F32),
        compiler_params=pltpu.CompilerParams(
            dimension_semantics=("arbitrary", "arbitrary"), vmem_limit_bytes=VMEM_LIMIT),
        name="merge",
    )(x, x, oa, ob, gpre, gpost, wu, wg, wa, wb, wc, wout, wpool, pscale)


def _mlp_kernel(x_ref, gpre_ref, gpost_ref, w1_ref, w2_ref, o_ref):
    x = x_ref[0]
    h = _rms(x, gpre_ref[...]).astype(BF16)
    y = None
    for c in range(D_FF // D_MODEL):
        cols = slice(c * D_MODEL, (c + 1) * D_MODEL)
        a = jnp.maximum(_dot(h, w1_ref[:, cols]), 0.0)
        part = _dot((a * a).astype(BF16), w2_ref[cols, :])
        y = part if y is None else y + part
    o_ref[0] = x + _rms(y, gpost_ref[...])


def _mlp(x, gpre, gpost, w1, w2):
    b, s, d = x.shape
    const = lambda shape: pl.BlockSpec(shape, lambda bi, ti: (0,) * len(shape), pipeline_mode=pl.Buffered(1))
    return pl.pallas_call(
        _mlp_kernel,
        grid=(b, s // TOK_TILE),
        in_specs=[pl.BlockSpec((1, TOK_TILE, d), lambda bi, ti: (bi, ti, 0)),
                  const((1, d)), const((1, d)), const((d, D_FF)), const((D_FF, d))],
        out_specs=pl.BlockSpec((1, TOK_TILE, d), lambda bi, ti: (bi, ti, 0)),
        out_shape=jax.ShapeDtypeStruct((b, s, d), F32),
        compiler_params=pltpu.CompilerParams(
            dimension_semantics=("arbitrary", "arbitrary"), vmem_limit_bytes=VMEM_LIMIT),
        name="mlp",
    )(x, gpre, gpost, w1, w2)


def _layer(x, w_in, w_br_a, w_br_b, w_br_c, w_out, w_pool, pool_scale,
           g_pre_mix, g_post_mix, g_pre_mlp, g_post_mlp, w_ff1, w_ff2):
    col = lambda k: slice(k * D_ATT, (k + 1) * D_ATT)
    wtok = w_in[:, D_ATT:6 * D_ATT]
    wtok = jnp.concatenate([wtok[:, :D_ATT], wtok[:, 2 * D_ATT:]], axis=1).astype(BF16)
    wft = jnp.concatenate([w_in[:, col(0)], w_in[:, col(2)]], axis=1).T.astype(BF16)
    wu = w_in[:, 6 * D_ATT:6 * D_ATT + POOL_WIDTH].astype(BF16)
    wg = w_in[:, 6 * D_ATT + POOL_WIDTH:].astype(BF16)
    row = lambda v: v.reshape(1, -1)

    ka, qkvb, qt, bias, vt = _in_proj(x, row(g_pre_mix), wtok, wft)
    oa = _moba(qt, bias, ka, vt)
    ob = _dilated(qkvb)
    x = _merge(x, oa, ob, row(g_pre_mix), row(g_post_mix), wu, wg,
               w_br_a.astype(BF16), w_br_b.astype(BF16), w_br_c.astype(BF16), w_out.astype(BF16),
               w_pool.astype(BF16), row(pool_scale))
    return _mlp(x, row(g_pre_mlp), row(g_post_mlp), w_ff1.astype(BF16), w_ff2.astype(BF16))


@jax.jit
def kernel(x, w_in, w_br_a, w_br_b, w_br_c, w_out, w_pool, pool_scale, g_pre_mix, g_post_mix, g_pre_mlp, g_post_mlp, w_ff1, w_ff2):
    for l in range(w_in.shape[0]):
        x = _layer(x, w_in[l], w_br_a[l], w_br_b[l], w_br_c[l], w_out[l], w_pool[l], pool_scale[l],
                   g_pre_mix[l], g_post_mix[l], g_pre_mlp[l], g_post_mlp[l], w_ff1[l], w_ff2[l])
    return x
```
